```python
import math
import jax, jax.numpy as jnp
from jax import lax
import numpy as np

D_MODEL = 1024
BATCH = 2
SEQ = 8192
DEPTH = 4

N_MIXERS = 3
N_ATTN_LAYERS = (DEPTH + 2) // 3
N_CONV_LAYERS = (DEPTH + 1) // 3
N_POOL_LAYERS = DEPTH // 3

N_HEADS = 8
HEAD_DIM = D_MODEL // N_HEADS // 2
ROT_DIM = HEAD_DIM // 4
ROPE_THETA = 500000.0
Q_BLOCK = 128
LAMBDA_STD = 0.1

CONV_WIDTH = 31

POOL_WINDOWS = (2, 4, 8, 16)
N_POOL_GROUPS = len(POOL_WINDOWS)
POOL_GROUP = D_MODEL // N_POOL_GROUPS

D_FF = int(math.ceil(8 * D_MODEL / 3 / 256) * 256)

ALPHA = float((2 * DEPTH) ** 0.25)
BETA = float((8 * DEPTH) ** -0.25)
LN_EPS = 1e-5

kernel_name = 'hybrid_diffattn_conformer_pool_deepnorm'


def _layer_norm(x, g, b):
    x32 = x.astype(jnp.float32)
    mu = jnp.mean(x32, axis=-1, keepdims=True)
    var = jnp.mean(jnp.square(x32 - mu), axis=-1, keepdims=True)
    y = (x32 - mu) * lax.rsqrt(var + LN_EPS) * g.astype(jnp.float32) + b.astype(jnp.float32)
    return y.astype(x.dtype)


def _rms_norm(x, g):
    x32 = x.astype(jnp.float32)
    y = x32 * lax.rsqrt(jnp.mean(jnp.square(x32), axis=-1, keepdims=True) + LN_EPS) * g.astype(jnp.float32)
    return y.astype(x.dtype)


def _rope_tables(seq, dtype):
    inv_freq = ROPE_THETA ** (-jnp.arange(0, ROT_DIM, 2, dtype=jnp.float32) / ROT_DIM)
    pos = jnp.arange(seq, dtype=jnp.float32)
    ang = pos[:, None] * inv_freq[None, :]
    return jnp.cos(ang).astype(dtype), jnp.sin(ang).astype(dtype)


def _partial_rope(x, cos, sin):
    half = ROT_DIM // 2
    c = cos[None, :, None, None, :]
    s = sin[None, :, None, None, :]
    x1 = x[..., :half]
    x2 = x[..., half:ROT_DIM]
    return jnp.concatenate([x1 * c - x2 * s, x2 * c + x1 * s, x[..., ROT_DIM:]], axis=-1)


def _diff_attention(x, wqkv, wo, lam, subln_g, cos, sin, lambda_init):
    B, S, _ = x.shape
    qkv = x @ wqkv
    q, k, v = jnp.split(qkv, 3, axis=-1)
    q = _partial_rope(q.reshape(B, S, N_HEADS, 2, HEAD_DIM), cos, sin)
    k = _partial_rope(k.reshape(B, S, N_HEADS, 2, HEAD_DIM), cos, sin)
    v = v.reshape(B, S, N_HEADS, 2 * HEAD_DIM)
    lam32 = lam.astype(jnp.float32)
    lam_full = (jnp.exp(jnp.sum(lam32[0] * lam32[1])) - jnp.exp(jnp.sum(lam32[2] * lam32[3]))
                + lambda_init)
    scale = HEAD_DIM ** -0.5
    n_blocks = S // Q_BLOCK
    q_blocks = q.reshape(B, n_blocks, Q_BLOCK, N_HEADS, 2, HEAD_DIM).transpose(1, 0, 2, 3, 4, 5)
    starts = jnp.arange(n_blocks, dtype=jnp.int32) * Q_BLOCK
    k_pos = jnp.arange(S, dtype=jnp.int32)

    def one_block(args):
        q_blk, start = args
        s = jnp.einsum('bqhmd,bkhmd->bhmqk', q_blk, k).astype(jnp.float32) * scale
        q_pos = start + jnp.arange(Q_BLOCK, dtype=jnp.int32)
        causal = k_pos[None, :] <= q_pos[:, None]
        s = jnp.where(causal, s, -1e30)
        p = jax.nn.softmax(s, axis=-1)
        a = p[:, :, 0] - lam_full * p[:, :, 1]
        return jnp.einsum('bhqk,bkhe->bqhe', a.astype(v.dtype), v)

    o = lax.map(one_block, (q_blocks, starts))
    o = o.transpose(1, 0, 2, 3, 4).reshape(B, S, N_HEADS, 2 * HEAD_DIM)
    o = _rms_norm(o, subln_g) * (1.0 - lambda_init)
    return o.reshape(B, S, N_HEADS * 2 * HEAD_DIM) @ wo


def _conformer_conv(x, pw1_w, pw1_b, dw_w, dw_b, ln_g, ln_b, pw2_w, pw2_b):
    h = x @ pw1_w + pw1_b
    a, gate = jnp.split(h, 2, axis=-1)
    h = a * jax.nn.sigmoid(gate)
    h = lax.conv_general_dilated(
        h, dw_w[:, None, :].astype(h.dtype), window_strides=(1,),
        padding=[(CONV_WIDTH - 1, 0)],
        dimension_numbers=('NWC', 'WIO', 'NWC'),
        feature_group_count=D_MODEL) + dw_b
    h = jax.nn.silu(_layer_norm(h, ln_g, ln_b))
    return h @ pw2_w + pw2_b


def _multiscale_pool(x, pool_w, pool_b, pool_scale):
    B, S, D = x.shape
    x32 = x.astype(jnp.float32)
    csum = jnp.concatenate([jnp.zeros((B, 1, D), jnp.float32), jnp.cumsum(x32, axis=1)], axis=1)
    t_idx = jnp.arange(S, dtype=jnp.int32)
    outs = []
    for g, w in enumerate(POOL_WINDOWS):
        cg = csum[..., g * POOL_GROUP:(g + 1) * POOL_GROUP]
        upper = cg[:, 1:]
        lower = jnp.concatenate([jnp.zeros((B, w - 1, POOL_GROUP), jnp.float32),
                                 cg[:, :S - w + 1]], axis=1)
        count = jnp.minimum(t_idx + 1, w).astype(jnp.float32)[None, :, None]
        outs.append((upper - lower) / count)
    pooled = (jnp.concatenate(outs, axis=-1) - x32).astype(x.dtype)
    y = jnp.einsum('bsgc,gcd->bsgd', pooled.reshape(B, S, N_POOL_GROUPS, POOL_GROUP), pool_w)
    return (y.reshape(B, S, D) + pool_b) * pool_scale


def _swiglu(x, w13, w2):
    h = x @ w13
    g, u = jnp.split(h, 2, axis=-1)
    return (jax.nn.silu(g) * u) @ w2


def setup_inputs(seed: int = 0) -> dict:
    key = jax.random.key(seed)
    ks = jax.random.split(key, 24)
    D = D_MODEL
    f32 = jnp.float32
    nrm = lambda k, shape, s: jax.random.normal(k, shape, f32) * s
    return {
        'x': nrm(ks[0], (BATCH, SEQ, D), 1.0),
        'attn_wqkv': nrm(ks[1], (N_ATTN_LAYERS, D, 3 * D), D ** -0.5),
        'attn_wo': nrm(ks[2], (N_ATTN_LAYERS, D, D), BETA * D ** -0.5),
        'attn_lambda': nrm(ks[3], (N_ATTN_LAYERS, 4, HEAD_DIM), LAMBDA_STD),
        'attn_subln_g': 1.0 + nrm(ks[4], (N_ATTN_LAYERS, 2 * HEAD_DIM), 0.02),
        'conv_pw1_w': nrm(ks[5], (N_CONV_LAYERS, D, 2 * D), D ** -0.5),
        'conv_pw1_b': nrm(ks[6], (N_CONV_LAYERS, 2 * D), 0.02),
        'conv_dw_w': nrm(ks[7], (N_CONV_LAYERS, CONV_WIDTH, D), CONV_WIDTH ** -0.5),
        'conv_dw_b': nrm(ks[8], (N_CONV_LAYERS, D), 0.02),
        'conv_ln_g': 1.0 + nrm(ks[9], (N_CONV_LAYERS, D), 0.02),
        'conv_ln_b': nrm(ks[10], (N_CONV_LAYERS, D), 0.02),
        'conv_pw2_w': nrm(ks[11], (N_CONV_LAYERS, D, D), BETA * D ** -0.5),
        'conv_pw2_b': nrm(ks[12], (N_CONV_LAYERS, D), 0.02),
        'pool_w': nrm(ks[13], (N_POOL_LAYERS, N_POOL_GROUPS, POOL_GROUP, POOL_GROUP), BETA * POOL_GROUP ** -0.5),
        'pool_b': nrm(ks[14], (N_POOL_LAYERS, D), 0.02),
        'pool_scale': 1.0 + nrm(ks[15], (N_POOL_LAYERS, D), 0.1),
        'ffn_w13': nrm(ks[16], (DEPTH, D, 2 * D_FF), D ** -0.5),
        'ffn_w2': nrm(ks[17], (DEPTH, D_FF, D), BETA * D_FF ** -0.5),
        'ln1_g': 1.0 + nrm(ks[18], (DEPTH, D), 0.02),
        'ln1_b': nrm(ks[19], (DEPTH, D), 0.02),
        'ln2_g': 1.0 + nrm(ks[20], (DEPTH, D), 0.02),
        'ln2_b': nrm(ks[21], (DEPTH, D), 0.02),
    }


def reference(x, attn_wqkv, attn_wo, attn_lambda, attn_subln_g,
              conv_pw1_w, conv_pw1_b, conv_dw_w, conv_dw_b, conv_ln_g, conv_ln_b,
              conv_pw2_w, conv_pw2_b, pool_w, pool_b, pool_scale,
              ffn_w13, ffn_w2, ln1_g, ln1_b, ln2_g, ln2_b):
    cos, sin = _rope_tables(x.shape[1], x.dtype)
    for i in range(DEPTH):
        mixer, j = i % N_MIXERS, i // N_MIXERS
        if mixer == 0:
            lambda_init = 0.8 - 0.6 * math.exp(-0.3 * i)
            h = _diff_attention(x, attn_wqkv[j], attn_wo[j], attn_lambda[j], attn_subln_g[j],
                                cos, sin, lambda_init)
        elif mixer == 1:
            h = _conformer_conv(x, conv_pw1_w[j], conv_pw1_b[j], conv_dw_w[j], conv_dw_b[j],
                                conv_ln_g[j], conv_ln_b[j], conv_pw2_w[j], conv_pw2_b[j])
        else:
            h = _multiscale_pool(x, pool_w[j], pool_b[j], pool_scale[j])
        x = _layer_norm(ALPHA * x + h, ln1_g[i], ln1_b[i])
        x = _layer_norm(ALPHA * x + _swiglu(x, ffn_w13[i], ffn_w2[i]), ln2_g[i], ln2_b[i])
    return x
```

```python
import functools
import math

import jax
import jax.numpy as jnp
from jax import lax
from jax.experimental import pallas as pl
from jax.experimental.pallas import tpu as pltpu

F32 = jnp.float32
BF16 = jnp.bfloat16

N_MIXERS = 3
N_HEADS = 8
ROPE_THETA = 500000.0
POOL_WINDOWS = (2, 4, 8, 16)
LN_EPS = 1e-5
NEG_BIG = -1e30

LANES = 128
VMEM_LIMIT = 56 * 1024 * 1024
ROW_TILE = 512
Q_TILE = 256
KV_TILE = 256
CONV_HALO = 32
POOL_HALO = 16


def _params(n_axes):
    return pltpu.CompilerParams(dimension_semantics=("arbitrary",) * n_axes,
                                vmem_limit_bytes=VMEM_LIMIT)


def _const_spec(shape):
    nd = len(shape)
    return pl.BlockSpec(shape, lambda *_: (0,) * nd)


def _dot(a, b):
    return jnp.dot(a, b, preferred_element_type=F32)


def _layer_norm(y, g, b):
    mu = jnp.mean(y, axis=-1, keepdims=True)
    yc = y - mu
    var = jnp.mean(yc * yc, axis=-1, keepdims=True)
    return yc * lax.rsqrt(var + LN_EPS) * g + b


def _qkv_rope_kernel(x_ref, w_ref, cos_ref, sa_ref, sb_ref, q_ref, k_ref, v_ref, *, q_scale):
    d = x_ref.shape[1]
    half_rot = 8
    xb = x_ref[...].astype(BF16)
    c = cos_ref[...]
    sa = sa_ref[...]
    sb = sb_ref[...]

    def rope(y):
        return (y * c + pltpu.roll(y, LANES - half_rot, 1) * sa
                + pltpu.roll(y, half_rot, 1) * sb)

    yq = _dot(xb, w_ref[:, 0:d])
    for h in range(d // LANES):
        sl = slice(h * LANES, (h + 1) * LANES)
        q_ref[:, sl] = (rope(yq[:, sl]) * q_scale).astype(BF16)
    yk = _dot(xb, w_ref[:, d:2 * d])
    for h in range(d // LANES):
        sl = slice(h * LANES, (h + 1) * LANES)
        k_ref[:, sl] = rope(yk[:, sl]).astype(BF16)
    v_ref[...] = _dot(xb, w_ref[:, 2 * d:3 * d]).astype(BF16)


def _qkv_rope(x, wqkv, cos_t, sa_t, sb_t, seq, q_scale):
    t, d = x.shape
    tm = min(ROW_TILE, seq)
    tiles_per_seq = seq // tm
    row = pl.BlockSpec((tm, d), lambda i: (i, 0))
    tab = pl.BlockSpec((tm, LANES), lambda i: (i % tiles_per_seq, 0))
    out = jax.ShapeDtypeStruct((t, d), BF16)
    return pl.pallas_call(
        functools.partial(_qkv_rope_kernel, q_scale=q_scale),
        grid=(t // tm,),
        in_specs=[row, _const_spec(wqkv.shape), tab, tab, tab],
        out_specs=[row, row, row],
        out_shape=[out, out, out],
        compiler_params=_params(1),
        name="qkv_rope",
    )(x, wqkv, cos_t, sa_t, sb_t)


def _attn_kernel(lam_ref, g_ref, q_ref, k_ref, v_ref, o_ref, *, tq, tk, lambda_init):
    qi = pl.program_id(2)
    hd = LANES // 2
    q = q_ref[0]
    lane = lax.broadcasted_iota(jnp.int32, (tq, LANES), 1)
    zero = jnp.zeros_like(q)
    qs = jnp.concatenate([jnp.where(lane < hd, q, zero), jnp.where(lane >= hd, q, zero)], axis=0)

    def step(j, carry, masked):
        m, l, acc = carry
        start = pl.multiple_of(j * tk, tk)
        kb = k_ref[0, pl.ds(start, tk), :]
        vb = v_ref[0, pl.ds(start, tk), :]
        s = lax.dot_general(qs, kb, (((1,), (1,)), ((), ())), preferred_element_type=F32)
        if masked:
            r = lax.broadcasted_iota(jnp.int32, (2 * tq, tk), 0)
            q_pos = qi * tq + jnp.where(r >= tq, r - tq, r)
            k_pos = start + lax.broadcasted_iota(jnp.int32, (2 * tq, tk), 1)
            s = jnp.where(k_pos <= q_pos, s, NEG_BIG)
        m_new = jnp.maximum(m, jnp.max(s, axis=-1, keepdims=True))
        p = jnp.exp(s - m_new)
        a = jnp.exp(m - m_new)
        l = a * l + jnp.sum(p, axis=-1, keepdims=True)
        acc = a * acc + _dot(p.astype(BF16), vb)
        return m_new, l, acc

    init = (jnp.full((2 * tq, 1), NEG_BIG, F32), jnp.zeros((2 * tq, 1), F32),
            jnp.zeros((2 * tq, LANES), F32))
    n_diag = tq // tk
    n_full = qi * n_diag
    carry = lax.fori_loop(0, n_full, functools.partial(step, masked=False), init)
    for dgn in range(n_diag):
        carry = step(n_full + dgn, carry, True)
    _, l, acc = carry

    lam = lam_ref[...]
    lam_full = (jnp.exp(jnp.sum(lam[0:1] * lam[1:2], axis=-1, keepdims=True))
                - jnp.exp(jnp.sum(lam[2:3] * lam[3:4], axis=-1, keepdims=True))
                + lambda_init)
    o = acc[:tq] / l[:tq] - lam_full * (acc[tq:] / l[tq:])
    o = o * lax.rsqrt(jnp.mean(o * o, axis=-1, keepdims=True) + LN_EPS) * g_ref[...]
    o_ref[0] = (o * (1.0 - lambda_init)).astype(BF16)


def _diff_attention(q, k, v, lam, subln_g, lambda_init):
    b, s, d = q.shape
    tq = min(Q_TILE, s)
    tk = min(KV_TILE, tq)
    n_heads = d // LANES
    q_spec = pl.BlockSpec((1, tq, LANES), lambda bi, h, qi: (bi, qi, h))
    kv_spec = pl.BlockSpec((1, s, LANES), lambda bi, h, qi: (bi, 0, h))
    return pl.pallas_call(
        functools.partial(_attn_kernel, tq=tq, tk=tk, lambda_init=lambda_init),
        grid=(b, n_heads, s // tq),
        in_specs=[_const_spec(lam.shape), _const_spec(subln_g.shape), q_spec, kv_spec, kv_spec],
        out_specs=q_spec,
        out_shape=jax.ShapeDtypeStruct((b, s, d), BF16),
        compiler_params=_params(3),
        name="diff_attn",
    )(lam, subln_g, q, k, v)


def _proj_res_ln_kernel(x_ref, a_ref, w_ref, g_ref, b_ref, o_ref, *, alpha):
    h = _dot(a_ref[...], w_ref[...])
    o_ref[...] = _layer_norm(alpha * x_ref[...] + h, g_ref[...], b_ref[...])


def _proj_res_ln(x, a, w, g, b, alpha):
    t, d = x.shape
    tm = min(ROW_TILE, t)
    row = pl.BlockSpec((tm, d), lambda i: (i, 0))
    a_spec = pl.BlockSpec((tm, a.shape[1]), lambda i: (i, 0))
    return pl.pallas_call(
        functools.partial(_proj_res_ln_kernel, alpha=alpha),
        grid=(t // tm,),
        in_specs=[row, a_spec, _const_spec(w.shape), _const_spec(g.shape), _const_spec(b.shape)],
        out_specs=row,
        out_shape=jax.ShapeDtypeStruct((t, d), F32),
        compiler_params=_params(1),
        name="attn_out_ln",
    )(x, a, w, g, b)


def _rope_lane_tables(seq):
    head_dim = LANES // 2
    rot = head_dim // 4
    half = rot // 2
    inv_freq = ROPE_THETA ** (-jnp.arange(0, rot, 2, dtype=F32) / rot)
    ang = jnp.arange(seq, dtype=F32)[:, None] * inv_freq[None, :]
    cos, sin = jnp.cos(ang), jnp.sin(ang)
    ones = jnp.ones((seq, head_dim - rot), F32)
    zeros = jnp.zeros((seq, head_dim - rot), F32)
    zh = jnp.zeros((seq, half), F32)
    c = jnp.concatenate([cos, cos, ones], axis=1)
    sa = jnp.concatenate([-sin, zh, zeros], axis=1)
    sb = jnp.concatenate([zh, sin, zeros], axis=1)
    two = lambda a: jnp.concatenate([a, a], axis=1)
    return two(c), two(sa), two(sb)


def _ffn_kernel(x_ref, w13_ref, w2_ref, g_ref, b_ref, o_ref, *, alpha):
    x = x_ref[...]
    xb = x.astype(BF16)
    n_chunks = w2_ref.shape[0]

    def body(c, acc):
        gate = _dot(xb, w13_ref[0, c])
        up = _dot(xb, w13_ref[1, c])
        act = (gate * jax.nn.sigmoid(gate) * up).astype(BF16)
        return acc + _dot(act, w2_ref[c])

    acc = lax.fori_loop(0, n_chunks, body, jnp.zeros(x.shape, F32))
    o_ref[...] = _layer_norm(alpha * x + acc, g_ref[...], b_ref[...])


def _ffn(x, w13c, w2c, g, b, alpha):
    t, d = x.shape
    tm = min(ROW_TILE, t)
    row = pl.BlockSpec((tm, d), lambda i: (i, 0))
    return pl.pallas_call(
        functools.partial(_ffn_kernel, alpha=alpha),
        grid=(t // tm,),
        in_specs=[row, _const_spec(w13c.shape), _const_spec(w2c.shape),
                  _const_spec(g.shape), _const_spec(b.shape)],
        out_specs=row,
        out_shape=jax.ShapeDtypeStruct((t, d), F32),
        compiler_params=_params(1),
        name="ffn",
    )(x, w13c, w2c, g, b)


def _ffn_chunk(d_ff):
    for c in (256, 128):
        if d_ff % c == 0:
            return c
    return d_ff


def _pw1_glu_kernel(x_ref, w_ref, b_ref, o_ref):
    d = x_ref.shape[1]
    h = _dot(x_ref[...].astype(BF16), w_ref[...]) + b_ref[...]
    o_ref[...] = h[:, :d] * jax.nn.sigmoid(h[:, d:])


def _pw1_glu(x, w, b):
    t, d = x.shape
    tm = min(ROW_TILE, t)
    row = pl.BlockSpec((tm, d), lambda i: (i, 0))
    return pl.pallas_call(
        _pw1_glu_kernel,
        grid=(t // tm,),
        in_specs=[row, _const_spec(w.shape), _const_spec(b.shape)],
        out_specs=row,
        out_shape=jax.ShapeDtypeStruct((t, d), F32),
        compiler_params=_params(1),
        name="conv_pw1_glu",
    )(x, w, b)


def _conv_kernel(x_ref, h_ref, halo_ref, dww_ref, dwb_ref, lng_ref, lnb_ref, w_ref, wb_ref,
                 g_ref, b_ref, o_ref, buf, *, alpha, tiles_per_seq):
    tm = x_ref.shape[0]
    width = dww_ref.shape[0]
    first = (pl.program_id(0) % tiles_per_seq) == 0
    halo = halo_ref[...]
    buf[0:CONV_HALO, :] = jnp.where(first, jnp.zeros_like(halo), halo)
    buf[CONV_HALO:, :] = h_ref[...]
    acc = jnp.zeros(h_ref.shape, F32) + dwb_ref[...]
    for j in range(width):
        acc = acc + buf[pl.ds(CONV_HALO - j, tm), :] * dww_ref[width - 1 - j:width - j, :]
    y = _layer_norm(acc, lng_ref[...], lnb_ref[...])
    y = (y * jax.nn.sigmoid(y)).astype(BF16)
    y = _dot(y, w_ref[...]) + wb_ref[...]
    o_ref[...] = _layer_norm(alpha * x_ref[...] + y, g_ref[...], b_ref[...])


def _conv_tail(x, h, dw_w, dw_b, ln_g, ln_b, pw2_w, pw2_b, g, b, alpha, seq):
    t, d = x.shape
    tm = min(ROW_TILE, seq)
    tiles_per_seq = seq // tm
    ratio = tm // CONV_HALO
    row = pl.BlockSpec((tm, d), lambda i: (i, 0))
    halo = pl.BlockSpec((CONV_HALO, d), lambda i: (jnp.maximum(i * ratio - 1, 0), 0))
    consts = [dw_w, dw_b, ln_g, ln_b, pw2_w, pw2_b, g, b]
    return pl.pallas_call(
        functools.partial(_conv_kernel, alpha=alpha, tiles_per_seq=tiles_per_seq),
        grid=(t // tm,),
        in_specs=[row, row, halo] + [_const_spec(c.shape) for c in consts],
        out_specs=row,
        out_shape=jax.ShapeDtypeStruct((t, d), F32),
        scratch_shapes=[pltpu.VMEM((CONV_HALO + tm, d), F32)],
        compiler_params=_params(1),
        name="conv_tail",
    )(x, h, h, *consts)


def _pool_kernel(x_ref, halo_ref, w_ref, pb_ref, ps_ref, g_ref, b_ref, o_ref, buf, *,
                 alpha, tiles_per_seq):
    tm, d = x_ref.shape
    n_groups = w_ref.shape[0]
    gw = d // n_groups
    tile = pl.program_id(0) % tiles_per_seq
    x = x_ref[...]
    halo = halo_ref[...]
    buf[0:POOL_HALO, :] = jnp.where(tile == 0, jnp.zeros_like(halo), halo)
    buf[POOL_HALO:, :] = x
    pos = tile * tm + lax.broadcasted_iota(jnp.int32, (tm, 1), 0)
    ys = []
    for gi, win in enumerate(POOL_WINDOWS):
        cols = slice(gi * gw, (gi + 1) * gw)
        tot = x[:, cols]
        for j in range(1, win):
            tot = tot + buf[pl.ds(POOL_HALO - j, tm), cols]
        count = jnp.minimum(pos + 1, win).astype(F32)
        pooled = (tot / count - x[:, cols]).astype(BF16)
        ys.append(_dot(pooled, w_ref[gi]))
    y = (jnp.concatenate(ys, axis=1) + pb_ref[...]) * ps_ref[...]
    o_ref[...] = _layer_norm(alpha * x + y, g_ref[...], b_ref[...])


def _pool_layer(x, pool_w, pool_b, pool_scale, g, b, alpha, seq):
    t, d = x.shape
    tm = min(ROW_TILE, seq)
    tiles_per_seq = seq // tm
    ratio = tm // POOL_HALO
    row = pl.BlockSpec((tm, d), lambda i: (i, 0))
    halo = pl.BlockSpec((POOL_HALO, d), lambda i: (jnp.maximum(i * ratio - 1, 0), 0))
    consts = [pool_w, pool_b, pool_scale, g, b]
    return pl.pallas_call(
        functools.partial(_pool_kernel, alpha=alpha, tiles_per_seq=tiles_per_seq),
        grid=(t // tm,),
        in_specs=[row, halo] + [_const_spec(c.shape) for c in consts],
        out_specs=row,
        out_shape=jax.ShapeDtypeStruct((t, d), F32),
        scratch_shapes=[pltpu.VMEM((POOL_HALO + tm, d), F32)],
        compiler_params=_params(1),
        name="pool_layer",
    )(x, x, *consts)


def kernel(x, attn_wqkv, attn_wo, attn_lambda, attn_subln_g, conv_pw1_w, conv_pw1_b, conv_dw_w, conv_dw_b, conv_ln_g, conv_ln_b, conv_pw2_w, conv_pw2_b, pool_w, pool_b, pool_scale, ffn_w13, ffn_w2, ln1_g, ln1_b, ln2_g, ln2_b):
    bsz, seq, d = x.shape
    depth = ffn_w13.shape[0]
    d_ff = ffn_w2.shape[1]
    alpha = float((2 * depth) ** 0.25)
    head_dim = d // N_HEADS // 2
    assert 2 * head_dim == LANES and seq % min(ROW_TILE, seq) == 0
    row2 = lambda v: v.reshape(1, -1)

    cos_t, sa_t, sb_t = _rope_lane_tables(seq)
    fc = _ffn_chunk(d_ff)
    n_chunks = d_ff // fc

    h = x.reshape(bsz * seq, d)
    for i in range(depth):
        mixer, j = i % N_MIXERS, i // N_MIXERS
        g1, b1 = row2(ln1_g[i]), row2(ln1_b[i])
        if mixer == 0:
            lambda_init = 0.8 - 0.6 * math.exp(-0.3 * i)
            q, k, v = _qkv_rope(h, attn_wqkv[j].astype(BF16), cos_t, sa_t, sb_t, seq,
                                head_dim ** -0.5)
            shp = (bsz, seq, d)
            o = _diff_attention(q.reshape(shp), k.reshape(shp), v.reshape(shp),
                                attn_lambda[j], row2(attn_subln_g[j]), lambda_init)
            h = _proj_res_ln(h, o.reshape(bsz * seq, d), attn_wo[j].astype(BF16), g1, b1, alpha)
        elif mixer == 1:
            glu = _pw1_glu(h, conv_pw1_w[j].astype(BF16), row2(conv_pw1_b[j]))
            h = _conv_tail(h, glu, conv_dw_w[j], row2(conv_dw_b[j]), row2(conv_ln_g[j]),
                           row2(conv_ln_b[j]), conv_pw2_w[j].astype(BF16), row2(conv_pw2_b[j]),
                           g1, b1, alpha, seq)
        else:
            h = _pool_layer(h, pool_w[j].astype(BF16), row2(pool_b[j]), row2(pool_scale[j]),
                            g1, b1, alpha, seq)
        w13c = ffn_w13[i].astype(BF16).reshape(d, 2, n_chunks, fc).transpose(1, 2, 0, 3)
        w2c = ffn_w2[i].astype(BF16).reshape(n_chunks, fc, d)
        h = _ffn(h, w13c, w2c, row2(ln2_g[i]), row2(ln2_b[i]), alpha)
    return h.reshape(bsz, seq, d)
```

```python
import functools
import math

import jax
import jax.numpy as jnp
from jax import lax
from jax.experimental import pallas as pl
from jax.experimental.pallas import tpu as pltpu

F32 = jnp.float32
BF16 = jnp.bfloat16

N_MIXERS = 3
N_HEADS = 8
ROPE_THETA = 500000.0
POOL_WINDOWS = (2, 4, 8, 16)
LN_EPS = 1e-5
NEG_BIG = -1e30

LANES = 128
VMEM_LIMIT = 56 * 1024 * 1024
ROW_TILE = 512
Q_TILE = 512
KV_TILE = 512
CONV_HALO = 32
POOL_HALO = 16


def _params(n_axes):
    return pltpu.CompilerParams(dimension_semantics=("arbitrary",) * n_axes,
                                vmem_limit_bytes=VMEM_LIMIT)


def _const_spec(shape):
    nd = len(shape)
    return pl.BlockSpec(shape, lambda *_: (0,) * nd)


def _dot(a, b):
    return jnp.dot(a, b, preferred_element_type=F32)


def _layer_norm(y, g, b):
    mu = jnp.mean(y, axis=-1, keepdims=True)
    yc = y - mu
    var = jnp.mean(yc * yc, axis=-1, keepdims=True)
    return yc * lax.rsqrt(var + LN_EPS) * g + b


def _qkv_rope_kernel(x_ref, w_ref, cos_ref, sa_ref, sb_ref, q_ref, k_ref, v_ref, *, q_scale):
    d = x_ref.shape[1]
    half_rot = 8
    xb = x_ref[...].astype(BF16)
    c = cos_ref[...]
    sa = sa_ref[...]
    sb = sb_ref[...]

    def rope(y):
        return (y * c + pltpu.roll(y, LANES - half_rot, 1) * sa
                + pltpu.roll(y, half_rot, 1) * sb)

    yq = _dot(xb, w_ref[:, 0:d])
    for h in range(d // LANES):
        sl = slice(h * LANES, (h + 1) * LANES)
        q_ref[:, sl] = (rope(yq[:, sl]) * q_scale).astype(BF16)
    yk = _dot(xb, w_ref[:, d:2 * d])
    for h in range(d // LANES):
        sl = slice(h * LANES, (h + 1) * LANES)
        k_ref[:, sl] = rope(yk[:, sl]).astype(BF16)
    v_ref[...] = _dot(xb, w_ref[:, 2 * d:3 * d]).astype(BF16)


def _qkv_rope(x, wqkv, cos_t, sa_t, sb_t, seq, q_scale):
    t, d = x.shape
    tm = min(ROW_TILE, seq)
    tiles_per_seq = seq // tm
    row = pl.BlockSpec((tm, d), lambda i: (i, 0))
    tab = pl.BlockSpec((tm, LANES), lambda i: (i % tiles_per_seq, 0))
    out = jax.ShapeDtypeStruct((t, d), BF16)
    return pl.pallas_call(
        functools.partial(_qkv_rope_kernel, q_scale=q_scale),
        grid=(t // tm,),
        in_specs=[row, _const_spec(wqkv.shape), tab, tab, tab],
        out_specs=[row, row, row],
        out_shape=[out, out, out],
        compiler_params=_params(1),
        name="qkv_rope",
    )(x, wqkv, cos_t, sa_t, sb_t)


def _attn_kernel(lam_ref, g_ref, q_ref, k_ref, v_ref, o_ref, m_sc, l_sc, acc_sc, *,
                 tq, tk, lambda_init):
    qi = pl.program_id(2)
    hd = LANES // 2
    n_col = tk // LANES
    q = q_ref[0]
    lane = lax.broadcasted_iota(jnp.int32, (tq, LANES), 1)
    zero = jnp.zeros_like(q)
    qs = (jnp.where(lane < hd, q, zero), jnp.where(lane >= hd, q, zero))
    m_sc[...] = jnp.full(m_sc.shape, NEG_BIG, F32)
    l_sc[...] = jnp.zeros(l_sc.shape, F32)
    acc_sc[...] = jnp.zeros(acc_sc.shape, F32)

    def step(j, masked):
        start = pl.multiple_of(j * tk, tk)
        kb = k_ref[0, pl.ds(start, tk), :]
        vb = v_ref[0, pl.ds(start, tk), :]
        if masked:
            q_pos = qi * tq + lax.broadcasted_iota(jnp.int32, (tq, tk), 0)
            k_pos = start + lax.broadcasted_iota(jnp.int32, (tq, tk), 1)
            keep = k_pos <= q_pos
        for c in range(2):
            s = lax.dot_general(qs[c], kb, (((1,), (1,)), ((), ())), preferred_element_type=F32)
            if masked:
                s = jnp.where(keep, s, NEG_BIG)
            cols = [s[:, t * LANES:(t + 1) * LANES] for t in range(n_col)]
            mt = cols[0]
            for t in range(1, n_col):
                mt = jnp.maximum(mt, cols[t])
            m_prev = m_sc[c]
            m_new = jnp.maximum(m_prev, jnp.max(mt, axis=-1, keepdims=True))
            a = jnp.exp2(m_prev - m_new)
            ps = [jnp.exp2(cols[t] - m_new) for t in range(n_col)]
            lt = ps[0]
            for t in range(1, n_col):
                lt = lt + ps[t]
            l_sc[c] = a * l_sc[c] + jnp.sum(lt, axis=-1, keepdims=True)
            p = jnp.concatenate(ps, axis=1).astype(BF16)
            acc_sc[c] = a * acc_sc[c] + _dot(p, vb)
            m_sc[c] = m_new

    n_diag = tq // tk
    n_full = qi * n_diag

    def body(j, carry):
        step(j, False)
        return carry

    lax.fori_loop(0, n_full, body, 0)
    for dgn in range(n_diag):
        step(n_full + dgn, True)

    lam = lam_ref[...]
    lam_full = (jnp.exp(jnp.sum(lam[0:1] * lam[1:2], axis=-1, keepdims=True))
                - jnp.exp(jnp.sum(lam[2:3] * lam[3:4], axis=-1, keepdims=True))
                + lambda_init)
    o = acc_sc[0] / l_sc[0] - lam_full * (acc_sc[1] / l_sc[1])
    o = o * lax.rsqrt(jnp.mean(o * o, axis=-1, keepdims=True) + LN_EPS) * g_ref[...]
    o_ref[0] = (o * (1.0 - lambda_init)).astype(BF16)


def _diff_attention(q, k, v, lam, subln_g, lambda_init):
    b, s, d = q.shape
    tq = min(Q_TILE, s)
    tk = min(KV_TILE, tq)
    n_heads = d // LANES
    q_spec = pl.BlockSpec((1, tq, LANES), lambda bi, h, qi: (bi, qi, h))
    kv_spec = pl.BlockSpec((1, s, LANES), lambda bi, h, qi: (bi, 0, h))
    return pl.pallas_call(
        functools.partial(_attn_kernel, tq=tq, tk=tk, lambda_init=lambda_init),
        grid=(b, n_heads, s // tq),
        in_specs=[_const_spec(lam.shape), _const_spec(subln_g.shape), q_spec, kv_spec, kv_spec],
        out_specs=q_spec,
        out_shape=jax.ShapeDtypeStruct((b, s, d), BF16),
        scratch_shapes=[pltpu.VMEM((2, tq, LANES), F32)] * 3,
        compiler_params=_params(3),
        name="diff_attn",
    )(lam, subln_g, q, k, v)


def _proj_res_ln_kernel(x_ref, a_ref, w_ref, g_ref, b_ref, o_ref, *, alpha):
    h = _dot(a_ref[...], w_ref[...])
    o_ref[...] = _layer_norm(alpha * x_ref[...] + h, g_ref[...], b_ref[...])


def _proj_res_ln(x, a, w, g, b, alpha):
    t, d = x.shape
    tm = min(ROW_TILE, t)
    row = pl.BlockSpec((tm, d), lambda i: (i, 0))
    a_spec = pl.BlockSpec((tm, a.shape[1]), lambda i: (i, 0))
    return pl.pallas_call(
        functools.partial(_proj_res_ln_kernel, alpha=alpha),
        grid=(t // tm,),
        in_specs=[row, a_spec, _const_spec(w.shape), _const_spec(g.shape), _const_spec(b.shape)],
        out_specs=row,
        out_shape=jax.ShapeDtypeStruct((t, d), F32),
        compiler_params=_params(1),
        name="attn_out_ln",
    )(x, a, w, g, b)


def _rope_lane_tables(seq):
    head_dim = LANES // 2
    rot = head_dim // 4
    half = rot // 2
    inv_freq = ROPE_THETA ** (-jnp.arange(0, rot, 2, dtype=F32) / rot)
    ang = jnp.arange(seq, dtype=F32)[:, None] * inv_freq[None, :]
    cos, sin = jnp.cos(ang), jnp.sin(ang)
    ones = jnp.ones((seq, head_dim - rot), F32)
    zeros = jnp.zeros((seq, head_dim - rot), F32)
    zh = jnp.zeros((seq, half), F32)
    c = jnp.concatenate([cos, cos, ones], axis=1)
    sa = jnp.concatenate([-sin, zh, zeros], axis=1)
    sb = jnp.concatenate([zh, sin, zeros], axis=1)
    two = lambda a: jnp.concatenate([a, a], axis=1)
    return two(c), two(sa), two(sb)


def _ffn_kernel(x_ref, w13_ref, w2_ref, g_ref, b_ref, o_ref, *, alpha):
    x = x_ref[...]
    xb = x.astype(BF16)
    n_chunks = w2_ref.shape[0]

    def body(c, acc):
        gate = _dot(xb, w13_ref[0, c])
        up = _dot(xb, w13_ref[1, c])
        act = (gate * jax.nn.sigmoid(gate) * up).astype(BF16)
        return acc + _dot(act, w2_ref[c])

    acc = lax.fori_loop(0, n_chunks, body, jnp.zeros(x.shape, F32))
    o_ref[...] = _layer_norm(alpha * x + acc, g_ref[...], b_ref[...])


def _ffn(x, w13c, w2c, g, b, alpha):
    t, d = x.shape
    tm = min(ROW_TILE, t)
    row = pl.BlockSpec((tm, d), lambda i: (i, 0))
    return pl.pallas_call(
        functools.partial(_ffn_kernel, alpha=alpha),
        grid=(t // tm,),
        in_specs=[row, _const_spec(w13c.shape), _const_spec(w2c.shape),
                  _const_spec(g.shape), _const_spec(b.shape)],
        out_specs=row,
        out_shape=jax.ShapeDtypeStruct((t, d), F32),
        compiler_params=_params(1),
        name="ffn",
    )(x, w13c, w2c, g, b)


def _ffn_chunk(d_ff):
    for c in (256, 128):
        if d_ff % c == 0:
            return c
    return d_ff


def _pw1_glu_kernel(x_ref, w_ref, b_ref, o_ref):
    d = x_ref.shape[1]
    h = _dot(x_ref[...].astype(BF16), w_ref[...]) + b_ref[...]
    o_ref[...] = h[:, :d] * jax.nn.sigmoid(h[:, d:])


def _pw1_glu(x, w, b):
    t, d = x.shape
    tm = min(ROW_TILE, t)
    row = pl.BlockSpec((tm, d), lambda i: (i, 0))
    return pl.pallas_call(
        _pw1_glu_kernel,
        grid=(t // tm,),
        in_specs=[row, _const_spec(w.shape), _const_spec(b.shape)],
        out_specs=row,
        out_shape=jax.ShapeDtypeStruct((t, d), F32),
        compiler_params=_params(1),
        name="conv_pw1_glu",
    )(x, w, b)


def _conv_kernel(x_ref, h_ref, halo_ref, dww_ref, dwb_ref, lng_ref, lnb_ref, w_ref, wb_ref,
                 g_ref, b_ref, o_ref, buf, *, alpha, tiles_per_seq):
    tm = x_ref.shape[0]
    width = dww_ref.shape[0]
    first = (pl.program_id(0) % tiles_per_seq) == 0
    halo = halo_ref[...]
    buf[0:CONV_HALO, :] = jnp.where(first, jnp.zeros_like(halo), halo)
    buf[CONV_HALO:, :] = h_ref[...]
    acc = jnp.zeros(h_ref.shape, F32) + dwb_ref[...]
    for j in range(width):
        acc = acc + buf[pl.ds(CONV_HALO - j, tm), :] * dww_ref[width - 1 - j:width - j, :]
    y = _layer_norm(acc, lng_ref[...], lnb_ref[...])
    y = (y * jax.nn.sigmoid(y)).astype(BF16)
    y = _dot(y, w_ref[...]) + wb_ref[...]
    o_ref[...] = _layer_norm(alpha * x_ref[...] + y, g_ref[...], b_ref[...])


def _conv_tail(x, h, dw_w, dw_b, ln_g, ln_b, pw2_w, pw2_b, g, b, alpha, seq):
    t, d = x.shape
    tm = min(ROW_TILE, seq)
    tiles_per_seq = seq // tm
    ratio = tm // CONV_HALO
    row = pl.BlockSpec((tm, d), lambda i: (i, 0))
    halo = pl.BlockSpec((CONV_HALO, d), lambda i: (jnp.maximum(i * ratio - 1, 0), 0))
    consts = [dw_w, dw_b, ln_g, ln_b, pw2_w, pw2_b, g, b]
    return pl.pallas_call(
        functools.partial(_conv_kernel, alpha=alpha, tiles_per_seq=tiles_per_seq),
        grid=(t // tm,),
        in_specs=[row, row, halo] + [_const_spec(c.shape) for c in consts],
        out_specs=row,
        out_shape=jax.ShapeDtypeStruct((t, d), F32),
        scratch_shapes=[pltpu.VMEM((CONV_HALO + tm, d), F32)],
        compiler_params=_params(1),
        name="conv_tail",
    )(x, h, h, *consts)


def _pool_kernel(x_ref, halo_ref, w_ref, pb_ref, ps_ref, g_ref, b_ref, o_ref, buf, *,
                 alpha, tiles_per_seq):
    tm, d = x_ref.shape
    n_groups = w_ref.shape[0]
    gw = d // n_groups
    tile = pl.program_id(0) % tiles_per_seq
    x = x_ref[...]
    halo = halo_ref[...]
    buf[0:POOL_HALO, :] = jnp.where(tile == 0, jnp.zeros_like(halo), halo)
    buf[POOL_HALO:, :] = x
    pos = tile * tm + lax.broadcasted_iota(jnp.int32, (tm, 1), 0)
    ys = []
    for gi, win in enumerate(POOL_WINDOWS):
        cols = slice(gi * gw, (gi + 1) * gw)
        tot = x[:, cols]
        for j in range(1, win):
            tot = tot + buf[pl.ds(POOL_HALO - j, tm), cols]
        count = jnp.minimum(pos + 1, win).astype(F32)
        pooled = (tot / count - x[:, cols]).astype(BF16)
        ys.append(_dot(pooled, w_ref[gi]))
    y = (jnp.concatenate(ys, axis=1) + pb_ref[...]) * ps_ref[...]
    o_ref[...] = _layer_norm(alpha * x + y, g_ref[...], b_ref[...])


def _pool_layer(x, pool_w, pool_b, pool_scale, g, b, alpha, seq):
    t, d = x.shape
    tm = min(ROW_TILE, seq)
    tiles_per_seq = seq // tm
    ratio = tm // POOL_HALO
    row = pl.BlockSpec((tm, d), lambda i: (i, 0))
    halo = pl.BlockSpec((POOL_HALO, d), lambda i: (jnp.maximum(i * ratio - 1, 0), 0))
    consts = [pool_w, pool_b, pool_scale, g, b]
    return pl.pallas_call(
        functools.partial(_pool_kernel, alpha=alpha, tiles_per_seq=tiles_per_seq),
        grid=(t // tm,),
        in_specs=[row, halo] + [_const_spec(c.shape) for c in consts],
        out_specs=row,
        out_shape=jax.ShapeDtypeStruct((t, d), F32),
        scratch_shapes=[pltpu.VMEM((POOL_HALO + tm, d), F32)],
        compiler_params=_params(1),
        name="pool_layer",
    )(x, x, *consts)


def kernel(x, attn_wqkv, attn_wo, attn_lambda, attn_subln_g, conv_pw1_w, conv_pw1_b, conv_dw_w, conv_dw_b, conv_ln_g, conv_ln_b, conv_pw2_w, conv_pw2_b, pool_w, pool_b, pool_scale, ffn_w13, ffn_w2, ln1_g, ln1_b, ln2_g, ln2_b):
    bsz, seq, d = x.shape
    depth = ffn_w13.shape[0]
    d_ff = ffn_w2.shape[1]
    alpha = float((2 * depth) ** 0.25)
    head_dim = d // N_HEADS // 2
    assert 2 * head_dim == LANES and seq % min(ROW_TILE, seq) == 0
    row2 = lambda v: v.reshape(1, -1)

    cos_t, sa_t, sb_t = _rope_lane_tables(seq)
    fc = _ffn_chunk(d_ff)
    n_chunks = d_ff // fc

    h = x.reshape(bsz * seq, d)
    for i in range(depth):
        mixer, j = i % N_MIXERS, i // N_MIXERS
        g1, b1 = row2(ln1_g[i]), row2(ln1_b[i])
        if mixer == 0:
            lambda_init = 0.8 - 0.6 * math.exp(-0.3 * i)
            q, k, v = _qkv_rope(h, attn_wqkv[j].astype(BF16), cos_t, sa_t, sb_t, seq,
                                head_dim ** -0.5 * math.log2(math.e))
            shp = (bsz, seq, d)
            o = _diff_attention(q.reshape(shp), k.reshape(shp), v.reshape(shp),
                                attn_lambda[j], row2(attn_subln_g[j]), lambda_init)
            h = _proj_res_ln(h, o.reshape(bsz * seq, d), attn_wo[j].astype(BF16), g1, b1, alpha)
        elif mixer == 1:
            glu = _pw1_glu(h, conv_pw1_w[j].astype(BF16), row2(conv_pw1_b[j]))
            h = _conv_tail(h, glu, conv_dw_w[j], row2(conv_dw_b[j]), row2(conv_ln_g[j]),
                           row2(conv_ln_b[j]), conv_pw2_w[j].astype(BF16), row2(conv_pw2_b[j]),
                           g1, b1, alpha, seq)
        else:
            h = _pool_layer(h, pool_w[j].astype(BF16), row2(pool_b[j]), row2(pool_scale[j]),
                            g1, b1, alpha, seq)
        w13c = ffn_w13[i].astype(BF16).reshape(d, 2, n_chunks, fc).transpose(1, 2, 0, 3)
        w2c = ffn_w2[i].astype(BF16).reshape(n_chunks, fc, d)
        h = _ffn(h, w13c, w2c, row2(ln2_g[i]), row2(ln2_b[i]), alpha)
    return h.reshape(bsz, seq, d)
```

```python
import functools
import math

import jax
import jax.numpy as jnp
from jax import lax
from jax.experimental import pallas as pl
from jax.experimental.pallas import tpu as pltpu

F32 = jnp.float32
BF16 = jnp.bfloat16

N_MIXERS = 3
N_HEADS = 8
ROPE_THETA = 500000.0
POOL_WINDOWS = (2, 4, 8, 16)
LN_EPS = 1e-5
NEG_BIG = -1e30

LANES = 128
VMEM_LIMIT = 56 * 1024 * 1024
ROW_TILE = 512
Q_TILE = 512
KV_TILE = 1024
CONV_HALO = 32
POOL_HALO = 16


def _params(n_axes):
    return pltpu.CompilerParams(dimension_semantics=("arbitrary",) * n_axes,
                                vmem_limit_bytes=VMEM_LIMIT)


def _const_spec(shape):
    nd = len(shape)
    return pl.BlockSpec(shape, lambda *_: (0,) * nd)


def _dot(a, b):
    return jnp.dot(a, b, preferred_element_type=F32)


def _layer_norm(y, g, b):
    mu = jnp.mean(y, axis=-1, keepdims=True)
    yc = y - mu
    var = jnp.mean(yc * yc, axis=-1, keepdims=True)
    return yc * lax.rsqrt(var + LN_EPS) * g + b


def _qkv_rope_kernel(x_ref, w_ref, cos_ref, sa_ref, sb_ref, q_ref, k_ref, v_ref, *, q_scale):
    d = x_ref.shape[1]
    half_rot = 8
    xb = x_ref[...].astype(BF16)
    c = cos_ref[...]
    sa = sa_ref[...]
    sb = sb_ref[...]

    def rope(y):
        return (y * c + pltpu.roll(y, LANES - half_rot, 1) * sa
                + pltpu.roll(y, half_rot, 1) * sb)

    yq = _dot(xb, w_ref[:, 0:d])
    for h in range(d // LANES):
        sl = slice(h * LANES, (h + 1) * LANES)
        q_ref[:, sl] = (rope(yq[:, sl]) * q_scale).astype(BF16)
    yk = _dot(xb, w_ref[:, d:2 * d])
    for h in range(d // LANES):
        sl = slice(h * LANES, (h + 1) * LANES)
        k_ref[:, sl] = rope(yk[:, sl]).astype(BF16)
    v_ref[...] = _dot(xb, w_ref[:, 2 * d:3 * d]).astype(BF16)


def _qkv_rope(x, wqkv, cos_t, sa_t, sb_t, seq, q_scale):
    t, d = x.shape
    tm = min(ROW_TILE, seq)
    tiles_per_seq = seq // tm
    row = pl.BlockSpec((tm, d), lambda i: (i, 0))
    tab = pl.BlockSpec((tm, LANES), lambda i: (i % tiles_per_seq, 0))
    out = jax.ShapeDtypeStruct((t, d), BF16)
    return pl.pallas_call(
        functools.partial(_qkv_rope_kernel, q_scale=q_scale),
        grid=(t // tm,),
        in_specs=[row, _const_spec(wqkv.shape), tab, tab, tab],
        out_specs=[row, row, row],
        out_shape=[out, out, out],
        compiler_params=_params(1),
        name="qkv_rope",
    )(x, wqkv, cos_t, sa_t, sb_t)


def _attn_kernel(lam_ref, g_ref, q_ref, k_ref, v_ref, o_ref, m_sc, l_sc, acc_sc, *,
                 tq, tk, lambda_init):
    qi = pl.program_id(2)
    hd = LANES // 2
    q = q_ref[0]
    lane = lax.broadcasted_iota(jnp.int32, (tq, LANES), 1)
    zero = jnp.zeros_like(q)
    qs = jnp.concatenate([jnp.where(lane < hd, q, zero), jnp.where(lane >= hd, q, zero)], axis=0)
    m_sc[...] = jnp.full(m_sc.shape, NEG_BIG, F32)
    l_sc[...] = jnp.zeros(l_sc.shape, F32)
    acc_sc[...] = jnp.zeros(acc_sc.shape, F32)

    def step(start, width, masked):
        n_col = width // LANES
        kb = k_ref[0, pl.ds(start, width), :]
        vb = v_ref[0, pl.ds(start, width), :]
        if masked:
            q_pos = qi * tq + lax.broadcasted_iota(jnp.int32, (tq, width), 0)
            k_pos = start + lax.broadcasted_iota(jnp.int32, (tq, width), 1)
            keep = k_pos <= q_pos
        s2 = lax.dot_general(qs, kb, (((1,), (1,)), ((), ())), preferred_element_type=F32)
        for c in range(2):
            s = s2[c * tq:(c + 1) * tq]
            if masked:
                s = jnp.where(keep, s, NEG_BIG)
            cols = [s[:, t * LANES:(t + 1) * LANES] for t in range(n_col)]
            mt = cols[0]
            for t in range(1, n_col):
                mt = jnp.maximum(mt, cols[t])
            m_prev = m_sc[c]
            m_new = jnp.maximum(m_prev, jnp.max(mt, axis=-1, keepdims=True))
            a = jnp.exp2(m_prev - m_new)
            ps = [jnp.exp2(cols[t] - m_new) for t in range(n_col)]
            lt = ps[0]
            for t in range(1, n_col):
                lt = lt + ps[t]
            l_sc[c] = a * l_sc[c] + jnp.sum(lt, axis=-1, keepdims=True)
            p = jnp.concatenate(ps, axis=1).astype(BF16)
            acc_sc[c] = a * acc_sc[c] + _dot(p, vb)
            m_sc[c] = m_new

    n_wide = (qi * tq) // tk
    n_left = qi - n_wide * (tk // tq)

    def wide(j, carry):
        step(pl.multiple_of(j * tk, tk), tk, False)
        return carry

    def left(j, carry):
        step(pl.multiple_of(n_wide * tk + j * tq, tq), tq, False)
        return carry

    lax.fori_loop(0, n_wide, wide, 0)
    lax.fori_loop(0, n_left, left, 0)
    step(pl.multiple_of(qi * tq, tq), tq, True)

    lam = lam_ref[...]
    lam_full = (jnp.exp(jnp.sum(lam[0:1] * lam[1:2], axis=-1, keepdims=True))
                - jnp.exp(jnp.sum(lam[2:3] * lam[3:4], axis=-1, keepdims=True))
                + lambda_init)
    o = acc_sc[0] / l_sc[0] - lam_full * (acc_sc[1] / l_sc[1])
    o = o * lax.rsqrt(jnp.mean(o * o, axis=-1, keepdims=True) + LN_EPS) * g_ref[...]
    o_ref[0] = (o * (1.0 - lambda_init)).astype(BF16)


def _diff_attention(q, k, v, lam, subln_g, lambda_init):
    b, s, d = q.shape
    tq = min(Q_TILE, s)
    tk = max(tq, min(KV_TILE, s))
    assert tk % tq == 0
    n_heads = d // LANES
    q_spec = pl.BlockSpec((1, tq, LANES), lambda bi, h, qi: (bi, qi, h))
    kv_spec = pl.BlockSpec((1, s, LANES), lambda bi, h, qi: (bi, 0, h))
    return pl.pallas_call(
        functools.partial(_attn_kernel, tq=tq, tk=tk, lambda_init=lambda_init),
        grid=(b, n_heads, s // tq),
        in_specs=[_const_spec(lam.shape), _const_spec(subln_g.shape), q_spec, kv_spec, kv_spec],
        out_specs=q_spec,
        out_shape=jax.ShapeDtypeStruct((b, s, d), BF16),
        scratch_shapes=[pltpu.VMEM((2, tq, LANES), F32)] * 3,
        compiler_params=_params(3),
        name="diff_attn",
    )(lam, subln_g, q, k, v)


def _proj_res_ln_kernel(x_ref, a_ref, w_ref, g_ref, b_ref, o_ref, *, alpha):
    h = _dot(a_ref[...], w_ref[...])
    o_ref[...] = _layer_norm(alpha * x_ref[...] + h, g_ref[...], b_ref[...])


def _proj_res_ln(x, a, w, g, b, alpha):
    t, d = x.shape
    tm = min(ROW_TILE, t)
    row = pl.BlockSpec((tm, d), lambda i: (i, 0))
    a_spec = pl.BlockSpec((tm, a.shape[1]), lambda i: (i, 0))
    return pl.pallas_call(
        functools.partial(_proj_res_ln_kernel, alpha=alpha),
        grid=(t // tm,),
        in_specs=[row, a_spec, _const_spec(w.shape), _const_spec(g.shape), _const_spec(b.shape)],
        out_specs=row,
        out_shape=jax.ShapeDtypeStruct((t, d), F32),
        compiler_params=_params(1),
        name="attn_out_ln",
    )(x, a, w, g, b)


def _rope_lane_tables(seq):
    head_dim = LANES // 2
    rot = head_dim // 4
    half = rot // 2
    lane = jnp.arange(LANES, dtype=jnp.int32) % head_dim
    inv_freq = ROPE_THETA ** (-(2 * (lane % half)).astype(F32) / rot)
    inv_freq = jnp.where(lane < rot, inv_freq, 0.0)
    ang = jnp.arange(seq, dtype=F32)[:, None] * inv_freq[None, :]
    c, s = jnp.cos(ang), jnp.sin(ang)
    sa = jnp.where(lane < half, -s, 0.0)
    sb = jnp.where((lane >= half) & (lane < rot), s, 0.0)
    return c, sa, sb


def _ffn_kernel(x_ref, w13_ref, w2_ref, g_ref, b_ref, o_ref, *, alpha, fc):
    x = x_ref[...]
    xb = x.astype(BF16)
    d_ff = w2_ref.shape[0]
    acc = None
    for c0 in range(0, d_ff, fc):
        gate = _dot(xb, w13_ref[:, c0:c0 + fc])
        up = _dot(xb, w13_ref[:, d_ff + c0:d_ff + c0 + fc])
        act = (gate * jax.nn.sigmoid(gate) * up).astype(BF16)
        part = _dot(act, w2_ref[c0:c0 + fc, :])
        acc = part if acc is None else acc + part
    o_ref[...] = _layer_norm(alpha * x + acc, g_ref[...], b_ref[...])


def _ffn(x, w13, w2, g, b, alpha):
    t, d = x.shape
    tm = min(ROW_TILE, t)
    row = pl.BlockSpec((tm, d), lambda i: (i, 0))
    return pl.pallas_call(
        functools.partial(_ffn_kernel, alpha=alpha, fc=_ffn_chunk(w2.shape[0])),
        grid=(t // tm,),
        in_specs=[row, _const_spec(w13.shape), _const_spec(w2.shape),
                  _const_spec(g.shape), _const_spec(b.shape)],
        out_specs=row,
        out_shape=jax.ShapeDtypeStruct((t, d), F32),
        compiler_params=_params(1),
        name="ffn",
    )(x, w13, w2, g, b)


def _ffn_chunk(d_ff):
    for c in (256, 128):
        if d_ff % c == 0:
            return c
    return d_ff


def _pw1_glu_kernel(x_ref, w_ref, b_ref, o_ref):
    d = x_ref.shape[1]
    h = _dot(x_ref[...].astype(BF16), w_ref[...]) + b_ref[...]
    o_ref[...] = h[:, :d] * jax.nn.sigmoid(h[:, d:])


def _pw1_glu(x, w, b):
    t, d = x.shape
    tm = min(ROW_TILE, t)
    row = pl.BlockSpec((tm, d), lambda i: (i, 0))
    return pl.pallas_call(
        _pw1_glu_kernel,
        grid=(t // tm,),
        in_specs=[row, _const_spec(w.shape), _const_spec(b.shape)],
        out_specs=row,
        out_shape=jax.ShapeDtypeStruct((t, d), F32),
        compiler_params=_params(1),
        name="conv_pw1_glu",
    )(x, w, b)


def _conv_kernel(x_ref, h_ref, halo_ref, dww_ref, dwb_ref, lng_ref, lnb_ref, w_ref, wb_ref,
                 g_ref, b_ref, o_ref, buf, shift_ref, *, alpha, tiles_per_seq):
    tm = x_ref.shape[0]
    width = dww_ref.shape[0]
    first = (pl.program_id(0) % tiles_per_seq) == 0
    halo = halo_ref[...]
    buf[0:CONV_HALO, :] = jnp.where(first, jnp.zeros_like(halo), halo)
    buf[CONV_HALO:, :] = h_ref[...]
    acc = jnp.zeros(h_ref.shape, F32) + dwb_ref[...]
    sub = 8
    n_a = (width + sub - 1) // sub
    top = sub * (n_a - 1)
    assert top + sub <= CONV_HALO
    for b in range(sub):
        if b:
            shift_ref[...] = buf[pl.ds(CONV_HALO - top - b, tm + top), :]
            src, base = shift_ref, top
        else:
            src, base = buf, CONV_HALO
        for a in range(n_a):
            j = sub * a + b
            if j < width:
                acc = acc + (src[pl.ds(base - sub * a, tm), :]
                             * dww_ref[width - 1 - j:width - j, :])
    y = _layer_norm(acc, lng_ref[...], lnb_ref[...])
    y = (y * jax.nn.sigmoid(y)).astype(BF16)
    y = _dot(y, w_ref[...]) + wb_ref[...]
    o_ref[...] = _layer_norm(alpha * x_ref[...] + y, g_ref[...], b_ref[...])


def _conv_tail(x, h, dw_w, dw_b, ln_g, ln_b, pw2_w, pw2_b, g, b, alpha, seq):
    t, d = x.shape
    tm = min(ROW_TILE, seq)
    tiles_per_seq = seq // tm
    ratio = tm // CONV_HALO
    row = pl.BlockSpec((tm, d), lambda i: (i, 0))
    halo = pl.BlockSpec((CONV_HALO, d), lambda i: (jnp.maximum(i * ratio - 1, 0), 0))
    consts = [dw_w, dw_b, ln_g, ln_b, pw2_w, pw2_b, g, b]
    return pl.pallas_call(
        functools.partial(_conv_kernel, alpha=alpha, tiles_per_seq=tiles_per_seq),
        grid=(t // tm,),
        in_specs=[row, row, halo] + [_const_spec(c.shape) for c in consts],
        out_specs=row,
        out_shape=jax.ShapeDtypeStruct((t, d), F32),
        scratch_shapes=[pltpu.VMEM((CONV_HALO + tm, d), F32),
                        pltpu.VMEM((tm + 8 * ((dw_w.shape[0] + 7) // 8 - 1), d), F32)],
        compiler_params=_params(1),
        name="conv_tail",
    )(x, h, h, *consts)


def _pool_kernel(x_ref, halo_ref, w_ref, pb_ref, ps_ref, g_ref, b_ref, o_ref, buf, *,
                 alpha, tiles_per_seq):
    tm, d = x_ref.shape
    n_groups = w_ref.shape[0]
    gw = d // n_groups
    tile = pl.program_id(0) % tiles_per_seq
    x = x_ref[...]
    halo = halo_ref[...]
    buf[0:POOL_HALO, :] = jnp.where(tile == 0, jnp.zeros_like(halo), halo)
    buf[POOL_HALO:, :] = x
    pos = tile * tm + lax.broadcasted_iota(jnp.int32, (tm, 1), 0)
    ys = []
    for gi, win in enumerate(POOL_WINDOWS):
        cols = slice(gi * gw, (gi + 1) * gw)
        tot = x[:, cols]
        for j in range(1, win):
            tot = tot + buf[pl.ds(POOL_HALO - j, tm), cols]
        count = jnp.minimum(pos + 1, win).astype(F32)
        pooled = (tot / count - x[:, cols]).astype(BF16)
        ys.append(_dot(pooled, w_ref[gi]))
    y = (jnp.concatenate(ys, axis=1) + pb_ref[...]) * ps_ref[...]
    o_ref[...] = _layer_norm(alpha * x + y, g_ref[...], b_ref[...])


def _pool_layer(x, pool_w, pool_b, pool_scale, g, b, alpha, seq):
    t, d = x.shape
    tm = min(ROW_TILE, seq)
    tiles_per_seq = seq // tm
    ratio = tm // POOL_HALO
    row = pl.BlockSpec((tm, d), lambda i: (i, 0))
    halo = pl.BlockSpec((POOL_HALO, d), lambda i: (jnp.maximum(i * ratio - 1, 0), 0))
    consts = [pool_w, pool_b, pool_scale, g, b]
    return pl.pallas_call(
        functools.partial(_pool_kernel, alpha=alpha, tiles_per_seq=tiles_per_seq),
        grid=(t // tm,),
        in_specs=[row, halo] + [_const_spec(c.shape) for c in consts],
        out_specs=row,
        out_shape=jax.ShapeDtypeStruct((t, d), F32),
        scratch_shapes=[pltpu.VMEM((POOL_HALO + tm, d), F32)],
        compiler_params=_params(1),
        name="pool_layer",
    )(x, x, *consts)


def kernel(x, attn_wqkv, attn_wo, attn_lambda, attn_subln_g, conv_pw1_w, conv_pw1_b, conv_dw_w, conv_dw_b, conv_ln_g, conv_ln_b, conv_pw2_w, conv_pw2_b, pool_w, pool_b, pool_scale, ffn_w13, ffn_w2, ln1_g, ln1_b, ln2_g, ln2_b):
    bsz, seq, d = x.shape
    depth = ffn_w13.shape[0]
    d_ff = ffn_w2.shape[1]
    alpha = float((2 * depth) ** 0.25)
    head_dim = d // N_HEADS // 2
    assert 2 * head_dim == LANES and seq % min(ROW_TILE, seq) == 0
    row2 = lambda v: v.reshape(1, -1)

    cos_t, sa_t, sb_t = _rope_lane_tables(seq)

    h = x.reshape(bsz * seq, d)
    for i in range(depth):
        mixer, j = i % N_MIXERS, i // N_MIXERS
        g1, b1 = row2(ln1_g[i]), row2(ln1_b[i])
        if mixer == 0:
            lambda_init = 0.8 - 0.6 * math.exp(-0.3 * i)
            q, k, v = _qkv_rope(h, attn_wqkv[j].astype(BF16), cos_t, sa_t, sb_t, seq,
                                head_dim ** -0.5 * math.log2(math.e))
            shp = (bsz, seq, d)
            o = _diff_attention(q.reshape(shp), k.reshape(shp), v.reshape(shp),
                                attn_lambda[j], row2(attn_subln_g[j]), lambda_init)
            h = _proj_res_ln(h, o.reshape(bsz * seq, d), attn_wo[j].astype(BF16), g1, b1, alpha)
        elif mixer == 1:
            glu = _pw1_glu(h, conv_pw1_w[j].astype(BF16), row2(conv_pw1_b[j]))
            h = _conv_tail(h, glu, conv_dw_w[j], row2(conv_dw_b[j]), row2(conv_ln_g[j]),
                           row2(conv_ln_b[j]), conv_pw2_w[j].astype(BF16), row2(conv_pw2_b[j]),
                           g1, b1, alpha, seq)
        else:
            h = _pool_layer(h, pool_w[j].astype(BF16), row2(pool_b[j]), row2(pool_scale[j]),
                            g1, b1, alpha, seq)
        h = _ffn(h, ffn_w13[i].astype(BF16), ffn_w2[i].astype(BF16),
                 row2(ln2_g[i]), row2(ln2_b[i]), alpha)
    return h.reshape(bsz, seq, d)
```

```python
import functools
import math

import jax
import jax.numpy as jnp
from jax import lax
from jax.experimental import pallas as pl
from jax.experimental.pallas import tpu as pltpu

F32 = jnp.float32
BF16 = jnp.bfloat16

N_MIXERS = 3
N_HEADS = 8
ROPE_THETA = 500000.0
POOL_WINDOWS = (2, 4, 8, 16)
LN_EPS = 1e-5
NEG_BIG = -1e30

LANES = 128
VMEM_LIMIT = 56 * 1024 * 1024
ROW_TILE = 512
Q_TILE = 512
KV_TILE = 1024
HEADS_PER_STEP = 2
CONV_HALO = 32
POOL_HALO = 16


def _params(n_axes, flags=None):
    return pltpu.CompilerParams(dimension_semantics=("arbitrary",) * n_axes,
                                vmem_limit_bytes=VMEM_LIMIT, flags=flags)


def _const_spec(shape):
    nd = len(shape)
    return pl.BlockSpec(shape, lambda *_: (0,) * nd)


def _dot(a, b):
    return jnp.dot(a, b, preferred_element_type=F32)


def _layer_norm(y, g, b):
    mu = jnp.mean(y, axis=-1, keepdims=True)
    yc = y - mu
    var = jnp.mean(yc * yc, axis=-1, keepdims=True)
    return yc * lax.rsqrt(var + LN_EPS) * g + b


def _qkv_rope_kernel(x_ref, w_ref, cos_ref, sa_ref, sb_ref, qt_ref, k_ref, vt_ref, *, q_scale):
    d = x_ref.shape[1]
    half_rot = 8
    xb = x_ref[...].astype(BF16)
    c = cos_ref[...]
    sa = sa_ref[...]
    sb = sb_ref[...]

    def rope(y):
        return (y * c + pltpu.roll(y, LANES - half_rot, 1) * sa
                + pltpu.roll(y, half_rot, 1) * sb)

    yq = _dot(xb, w_ref[:, 0:d])
    for h in range(d // LANES):
        sl = slice(h * LANES, (h + 1) * LANES)
        qt_ref[0, sl, :] = (rope(yq[:, sl]) * q_scale).T.astype(BF16)
    yk = _dot(xb, w_ref[:, d:2 * d])
    for h in range(d // LANES):
        sl = slice(h * LANES, (h + 1) * LANES)
        k_ref[:, sl] = rope(yk[:, sl]).astype(BF16)
    yv = _dot(xb, w_ref[:, 2 * d:3 * d])
    for h in range(d // LANES):
        sl = slice(h * LANES, (h + 1) * LANES)
        vt_ref[0, sl, :] = yv[:, sl].T.astype(BF16)


def _qkv_rope(x, wqkv, cos_t, sa_t, sb_t, bsz, seq, q_scale):
    t, d = x.shape
    tm = min(ROW_TILE, seq)
    tiles_per_seq = seq // tm
    row = pl.BlockSpec((tm, d), lambda i: (i, 0))
    col = pl.BlockSpec((1, d, tm), lambda i: (i // tiles_per_seq, 0, i % tiles_per_seq))
    tab = pl.BlockSpec((tm, LANES), lambda i: (i % tiles_per_seq, 0))
    out_t = jax.ShapeDtypeStruct((bsz, d, seq), BF16)
    return pl.pallas_call(
        functools.partial(_qkv_rope_kernel, q_scale=q_scale),
        grid=(t // tm,),
        in_specs=[row, _const_spec(wqkv.shape), tab, tab, tab],
        out_specs=[col, row, col],
        out_shape=[out_t, jax.ShapeDtypeStruct((t, d), BF16), out_t],
        compiler_params=_params(1),
        name="qkv_rope",
    )(x, wqkv, cos_t, sa_t, sb_t)


def _attn_kernel(lam_ref, g_ref, qt_ref, k_ref, vt_ref, o_ref, m_sc, l_sc, acc_sc, *,
                 tq, tk, n_h, lambda_init):
    qi = pl.program_id(2)
    hd = LANES // 2
    cw = min(2 * LANES, tq)
    feat =lax.broadcasted_iota(jnp.int32, (LANES, tq), 0)
    qts = []
    for hh in range(n_h):
        qt = qt_ref[0, hh * LANES:(hh + 1) * LANES, :]
        zero = jnp.zeros_like(qt)
        qts.append(jnp.concatenate([jnp.where(feat < hd, qt, zero), jnp.where(feat >= hd, qt, zero)],
                                   axis=1))
    m_sc[...] = jnp.full(m_sc.shape, NEG_BIG, F32)
    l_sc[...] = jnp.zeros(l_sc.shape, F32)
    acc_sc[...] = jnp.zeros(acc_sc.shape, F32)

    def step(start, width, masked):
        if masked:
            k_pos = start + lax.broadcasted_iota(jnp.int32, (width, tq), 0)
            q_pos = qi * tq + lax.broadcasted_iota(jnp.int32, (width, tq), 1)
            keep = k_pos <= q_pos
        scores = []
        for hh in range(n_h):
            kb = k_ref[0, pl.ds(start, width), hh * LANES:(hh + 1) * LANES]
            scores.append(_dot(kb, qts[hh]))
        for hh in range(n_h):
            vtb = vt_ref[0, hh * LANES:(hh + 1) * LANES, pl.ds(start, width)]
            for c, q0 in [(c, q0) for c in range(2) for q0 in range(0, tq, cw)]:
                i = 2 * hh + c
                qs_ = slice(q0, q0 + cw)
                s = scores[hh][:, c * tq + q0:c * tq + q0 + cw]
                if masked:
                    s = jnp.where(keep[:, qs_], s, NEG_BIG)
                m_prev = m_sc[i, :, qs_]
                m_new = jnp.maximum(m_prev, jnp.max(s, axis=0, keepdims=True))
                a = jnp.exp2(m_prev - m_new)
                p = jnp.exp2(s - m_new)
                l_sc[i, :, qs_] = a * l_sc[i, :, qs_] + jnp.sum(p, axis=0, keepdims=True)
                acc_sc[i, :, qs_] = a * acc_sc[i, :, qs_] + _dot(vtb, p.astype(BF16))
                m_sc[i, :, qs_] = m_new

    n_wide = (qi * tq) // tk
    n_left = qi - n_wide * (tk // tq)

    def wide(j, carry):
        step(pl.multiple_of(j * tk, tk), tk, False)
        return carry

    def left(j, carry):
        step(pl.multiple_of(n_wide * tk + j * tq, tq), tq, False)
        return carry

    lax.fori_loop(0, n_wide, wide, 0)
    lax.fori_loop(0, n_left, left, 0)
    step(pl.multiple_of(qi * tq, tq), tq, True)

    lam = lam_ref[...]
    lam_full = (jnp.exp(jnp.sum(lam[0:1] * lam[1:2], axis=-1, keepdims=True))
                - jnp.exp(jnp.sum(lam[2:3] * lam[3:4], axis=-1, keepdims=True))
                + lambda_init)
    for hh in range(n_h):
        ot = (acc_sc[2 * hh] / l_sc[2 * hh]
              - lam_full * (acc_sc[2 * hh + 1] / l_sc[2 * hh + 1]))
        ot = ot * lax.rsqrt(jnp.mean(ot * ot, axis=0, keepdims=True) + LN_EPS) * g_ref[...]
        o_ref[0, :, hh * LANES:(hh + 1) * LANES] = (ot * (1.0 - lambda_init)).T.astype(BF16)


def _diff_attention(qt, k, vt, lam, subln_g, lambda_init):
    b, s, d = k.shape
    tq = min(Q_TILE, s)
    tk = max(tq, min(KV_TILE, s))
    assert tk % tq == 0
    n_h = HEADS_PER_STEP
    gw = n_h * LANES
    assert d % gw == 0
    qt_spec = pl.BlockSpec((1, gw, tq), lambda bi, h, qi: (bi, h, qi))
    k_spec = pl.BlockSpec((1, s, gw), lambda bi, h, qi: (bi, 0, h))
    vt_spec = pl.BlockSpec((1, gw, s), lambda bi, h, qi: (bi, h, 0))
    o_spec = pl.BlockSpec((1, tq, gw), lambda bi, h, qi: (bi, qi, h))
    g_col = subln_g.reshape(LANES, 1)
    return pl.pallas_call(
        functools.partial(_attn_kernel, tq=tq, tk=tk, n_h=n_h, lambda_init=lambda_init),
        grid=(b, d // gw, s // tq),
        in_specs=[_const_spec(lam.shape), _const_spec(g_col.shape), qt_spec, k_spec, vt_spec],
        out_specs=o_spec,
        out_shape=jax.ShapeDtypeStruct((b, s, d), BF16),
        scratch_shapes=[pltpu.VMEM((2 * n_h, 1, tq), F32), pltpu.VMEM((2 * n_h, 1, tq), F32),
                        pltpu.VMEM((2 * n_h, LANES, tq), F32)],
        compiler_params=_params(3),
        name="diff_attn",
    )(lam, g_col, qt, k, vt)


def _proj_res_ln_kernel(x_ref, a_ref, w_ref, g_ref, b_ref, o_ref, *, alpha):
    h = _dot(a_ref[...], w_ref[...])
    o_ref[...] = _layer_norm(alpha * x_ref[...] + h, g_ref[...], b_ref[...])


def _proj_res_ln(x, a, w, g, b, alpha):
    t, d = x.shape
    tm = min(ROW_TILE, t)
    row = pl.BlockSpec((tm, d), lambda i: (i, 0))
    a_spec = pl.BlockSpec((tm, a.shape[1]), lambda i: (i, 0))
    return pl.pallas_call(
        functools.partial(_proj_res_ln_kernel, alpha=alpha),
        grid=(t // tm,),
        in_specs=[row, a_spec, _const_spec(w.shape), _const_spec(g.shape), _const_spec(b.shape)],
        out_specs=row,
        out_shape=jax.ShapeDtypeStruct((t, d), F32),
        compiler_params=_params(1),
        name="attn_out_ln",
    )(x, a, w, g, b)


def _rope_lane_tables(seq):
    head_dim = LANES // 2
    rot = head_dim // 4
    half = rot // 2
    lane = jnp.arange(LANES, dtype=jnp.int32) % head_dim
    inv_freq = ROPE_THETA ** (-(2 * (lane % half)).astype(F32) / rot)
    inv_freq = jnp.where(lane < rot, inv_freq, 0.0)
    ang = jnp.arange(seq, dtype=F32)[:, None] * inv_freq[None, :]
    c, s = jnp.cos(ang), jnp.sin(ang)
    sa = jnp.where(lane < half, -s, 0.0)
    sb = jnp.where((lane >= half) & (lane < rot), s, 0.0)
    return c, sa, sb


def _ffn_kernel(x_ref, w13_ref, w2_ref, g_ref, b_ref, o_ref, *, alpha, fc):
    x = x_ref[...]
    xb = x.astype(BF16)
    d_ff = w2_ref.shape[0]
    acc = None
    for c0 in range(0, d_ff, fc):
        gate = _dot(xb, w13_ref[:, c0:c0 + fc])
        up = _dot(xb, w13_ref[:, d_ff + c0:d_ff + c0 + fc])
        act = (gate * jax.nn.sigmoid(gate) * up).astype(BF16)
        part = _dot(act, w2_ref[c0:c0 + fc, :])
        acc = part if acc is None else acc + part
    o_ref[...] = _layer_norm(alpha * x + acc, g_ref[...], b_ref[...])


def _ffn(x, w13, w2, g, b, alpha):
    t, d = x.shape
    tm = min(ROW_TILE, t)
    row = pl.BlockSpec((tm, d), lambda i: (i, 0))
    return pl.pallas_call(
        functools.partial(_ffn_kernel, alpha=alpha, fc=_ffn_chunk(w2.shape[0])),
        grid=(t // tm,),
        in_specs=[row, _const_spec(w13.shape), _const_spec(w2.shape),
                  _const_spec(g.shape), _const_spec(b.shape)],
        out_specs=row,
        out_shape=jax.ShapeDtypeStruct((t, d), F32),
        compiler_params=_params(1),
        name="ffn",
    )(x, w13, w2, g, b)


def _ffn_chunk(d_ff):
    for c in (256, 128):
        if d_ff % c == 0:
            return c
    return d_ff


def _pw1_glu_kernel(x_ref, w_ref, b_ref, o_ref):
    d = x_ref.shape[1]
    h = _dot(x_ref[...].astype(BF16), w_ref[...]) + b_ref[...]
    o_ref[...] = h[:, :d] * jax.nn.sigmoid(h[:, d:])


def _pw1_glu(x, w, b):
    t, d = x.shape
    tm = min(ROW_TILE, t)
    row = pl.BlockSpec((tm, d), lambda i: (i, 0))
    return pl.pallas_call(
        _pw1_glu_kernel,
        grid=(t // tm,),
        in_specs=[row, _const_spec(w.shape), _const_spec(b.shape)],
        out_specs=row,
        out_shape=jax.ShapeDtypeStruct((t, d), F32),
        compiler_params=_params(1),
        name="conv_pw1_glu",
    )(x, w, b)


def _conv_kernel(x_ref, h_ref, halo_ref, dww_ref, dwb_ref, lng_ref, lnb_ref, w_ref, wb_ref,
                 g_ref, b_ref, o_ref, buf, shift_ref, *, alpha, tiles_per_seq):
    tm = x_ref.shape[0]
    width = dww_ref.shape[0]
    first = (pl.program_id(0) % tiles_per_seq) == 0
    halo = halo_ref[...]
    buf[0:CONV_HALO, :] = jnp.where(first, jnp.zeros_like(halo), halo)
    buf[CONV_HALO:, :] = h_ref[...]
    acc = jnp.zeros(h_ref.shape, F32) + dwb_ref[...]
    sub = 8
    n_a = (width + sub - 1) // sub
    top = sub * (n_a - 1)
    assert top + sub <= CONV_HALO
    for b in range(sub):
        if b:
            shift_ref[...] = buf[pl.ds(CONV_HALO - top - b, tm + top), :]
            src, base = shift_ref, top
        else:
            src, base = buf, CONV_HALO
        for a in range(n_a):
            j = sub * a + b
            if j < width:
                acc = acc + (src[pl.ds(base - sub * a, tm), :]
                             * dww_ref[width - 1 - j:width - j, :])
    y = _layer_norm(acc, lng_ref[...], lnb_ref[...])
    y = (y * jax.nn.sigmoid(y)).astype(BF16)
    y = _dot(y, w_ref[...]) + wb_ref[...]
    o_ref[...] = _layer_norm(alpha * x_ref[...] + y, g_ref[...], b_ref[...])


def _conv_tail(x, h, dw_w, dw_b, ln_g, ln_b, pw2_w, pw2_b, g, b, alpha, seq):
    t, d = x.shape
    tm = min(ROW_TILE, seq)
    tiles_per_seq = seq // tm
    ratio = tm // CONV_HALO
    row = pl.BlockSpec((tm, d), lambda i: (i, 0))
    halo = pl.BlockSpec((CONV_HALO, d), lambda i: (jnp.maximum(i * ratio - 1, 0), 0))
    consts = [dw_w, dw_b, ln_g, ln_b, pw2_w, pw2_b, g, b]
    return pl.pallas_call(
        functools.partial(_conv_kernel, alpha=alpha, tiles_per_seq=tiles_per_seq),
        grid=(t // tm,),
        in_specs=[row, row, halo] + [_const_spec(c.shape) for c in consts],
        out_specs=row,
        out_shape=jax.ShapeDtypeStruct((t, d), F32),
        scratch_shapes=[pltpu.VMEM((CONV_HALO + tm, d), F32),
                        pltpu.VMEM((tm + 8 * ((dw_w.shape[0] + 7) // 8 - 1), d), F32)],
        compiler_params=_params(1),
        name="conv_tail",
    )(x, h, h, *consts)


def _pool_kernel(x_ref, halo_ref, w_ref, pb_ref, ps_ref, g_ref, b_ref, o_ref, buf, *,
                 alpha, tiles_per_seq):
    tm, d = x_ref.shape
    n_groups = w_ref.shape[0]
    gw = d // n_groups
    tile = pl.program_id(0) % tiles_per_seq
    x = x_ref[...]
    halo = halo_ref[...]
    buf[0:POOL_HALO, :] = jnp.where(tile == 0, jnp.zeros_like(halo), halo)
    buf[POOL_HALO:, :] = x
    pos = tile * tm + lax.broadcasted_iota(jnp.int32, (tm, 1), 0)
    ys = []
    for gi, win in enumerate(POOL_WINDOWS):
        cols = slice(gi * gw, (gi + 1) * gw)
        tot = x[:, cols]
        for j in range(1, win):
            tot = tot + buf[pl.ds(POOL_HALO - j, tm), cols]
        count = jnp.minimum(pos + 1, win).astype(F32)
        pooled = (tot / count - x[:, cols]).astype(BF16)
        ys.append(_dot(pooled, w_ref[gi]))
    y = (jnp.concatenate(ys, axis=1) + pb_ref[...]) * ps_ref[...]
    o_ref[...] = _layer_norm(alpha * x + y, g_ref[...], b_ref[...])


def _pool_layer(x, pool_w, pool_b, pool_scale, g, b, alpha, seq):
    t, d = x.shape
    tm = min(ROW_TILE, seq)
    tiles_per_seq = seq // tm
    ratio = tm // POOL_HALO
    row = pl.BlockSpec((tm, d), lambda i: (i, 0))
    halo = pl.BlockSpec((POOL_HALO, d), lambda i: (jnp.maximum(i * ratio - 1, 0), 0))
    consts = [pool_w, pool_b, pool_scale, g, b]
    return pl.pallas_call(
        functools.partial(_pool_kernel, alpha=alpha, tiles_per_seq=tiles_per_seq),
        grid=(t // tm,),
        in_specs=[row, halo] + [_const_spec(c.shape) for c in consts],
        out_specs=row,
        out_shape=jax.ShapeDtypeStruct((t, d), F32),
        scratch_shapes=[pltpu.VMEM((POOL_HALO + tm, d), F32)],
        compiler_params=_params(1),
        name="pool_layer",
    )(x, x, *consts)


def kernel(x, attn_wqkv, attn_wo, attn_lambda, attn_subln_g, conv_pw1_w, conv_pw1_b, conv_dw_w, conv_dw_b, conv_ln_g, conv_ln_b, conv_pw2_w, conv_pw2_b, pool_w, pool_b, pool_scale, ffn_w13, ffn_w2, ln1_g, ln1_b, ln2_g, ln2_b):
    bsz, seq, d = x.shape
    depth = ffn_w13.shape[0]
    d_ff = ffn_w2.shape[1]
    alpha = float((2 * depth) ** 0.25)
    head_dim = d // N_HEADS // 2
    assert 2 * head_dim == LANES and seq % min(ROW_TILE, seq) == 0
    row2 = lambda v: v.reshape(1, -1)

    cos_t, sa_t, sb_t = _rope_lane_tables(seq)

    h = x.reshape(bsz * seq, d)
    for i in range(depth):
        mixer, j = i % N_MIXERS, i // N_MIXERS
        g1, b1 = row2(ln1_g[i]), row2(ln1_b[i])
        if mixer == 0:
            lambda_init = 0.8 - 0.6 * math.exp(-0.3 * i)
            qt, k, vt = _qkv_rope(h, attn_wqkv[j].astype(BF16), cos_t, sa_t, sb_t, bsz, seq,
                                  head_dim ** -0.5 * math.log2(math.e))
            o = _diff_attention(qt, k.reshape(bsz, seq, d), vt,
                                attn_lambda[j], attn_subln_g[j], lambda_init)
            h = _proj_res_ln(h, o.reshape(bsz * seq, d), attn_wo[j].astype(BF16), g1, b1, alpha)
        elif mixer == 1:
            glu = _pw1_glu(h, conv_pw1_w[j].astype(BF16), row2(conv_pw1_b[j]))
            h = _conv_tail(h, glu, conv_dw_w[j], row2(conv_dw_b[j]), row2(conv_ln_g[j]),
                           row2(conv_ln_b[j]), conv_pw2_w[j].astype(BF16), row2(conv_pw2_b[j]),
                           g1, b1, alpha, seq)
        else:
            h = _pool_layer(h, pool_w[j].astype(BF16), row2(pool_b[j]), row2(pool_scale[j]),
                            g1, b1, alpha, seq)
        h = _ffn(h, ffn_w13[i].astype(BF16), ffn_w2[i].astype(BF16),
                 row2(ln2_g[i]), row2(ln2_b[i]), alpha)
    return h.reshape(bsz, seq, d)
```

```python
import functools
import math

import jax
import jax.numpy as jnp
from jax import lax
from jax.experimental import pallas as pl
from jax.experimental.pallas import tpu as pltpu

F32 = jnp.float32
BF16 = jnp.bfloat16

N_MIXERS = 3
N_HEADS = 8
ROPE_THETA = 500000.0
POOL_WINDOWS = (2, 4, 8, 16)
LN_EPS = 1e-5
NEG_BIG = -1e30

LANES = 128
VMEM_LIMIT = 56 * 1024 * 1024
ROW_TILE = 512
Q_TILE = 512
KV_TILE = 1024
HEADS_PER_STEP = 2
SUM_ROWS = 16
V_ROWS = LANES + SUM_ROWS
CONV_HALO = 32
POOL_HALO = 16


def _params(n_axes, flags=None):
    return pltpu.CompilerParams(dimension_semantics=("arbitrary",) * n_axes,
                                vmem_limit_bytes=VMEM_LIMIT, flags=flags)


def _const_spec(shape):
    nd = len(shape)
    return pl.BlockSpec(shape, lambda *_: (0,) * nd, pipeline_mode=pl.Buffered(1))


def _dot(a, b):
    return jnp.dot(a, b, preferred_element_type=F32)


def _layer_norm(y, g, b):
    mu = jnp.mean(y, axis=-1, keepdims=True)
    yc = y - mu
    var = jnp.mean(yc * yc, axis=-1, keepdims=True)
    return yc * lax.rsqrt(var + LN_EPS) * g + b


def _qkv_rope_kernel(x_ref, w_ref, cos_ref, sa_ref, sb_ref, qt_ref, k_ref, vt_ref, *, q_scale):
    d = x_ref.shape[1]
    half_rot = 8
    xb = x_ref[...].astype(BF16)
    c = cos_ref[...]
    sa = sa_ref[...]
    sb = sb_ref[...]

    def rope(y):
        return (y * c + pltpu.roll(y, LANES - half_rot, 1) * sa
                + pltpu.roll(y, half_rot, 1) * sb)

    yq = _dot(xb, w_ref[:, 0:d])
    for h in range(d // LANES):
        sl = slice(h * LANES, (h + 1) * LANES)
        qt_ref[0, sl, :] = (rope(yq[:, sl]) * q_scale).T.astype(BF16)
    yk = _dot(xb, w_ref[:, d:2 * d])
    for h in range(d // LANES):
        sl = slice(h * LANES, (h + 1) * LANES)
        k_ref[:, sl] = rope(yk[:, sl]).astype(BF16)
    yv = _dot(xb, w_ref[:, 2 * d:3 * d])
    ones = jnp.ones((SUM_ROWS, x_ref.shape[0]), BF16)
    for h in range(d // LANES):
        r0 = h * V_ROWS
        vt_ref[0, r0:r0 + LANES, :] = yv[:, h * LANES:(h + 1) * LANES].T.astype(BF16)
        vt_ref[0, r0 + LANES:r0 + V_ROWS, :] = ones


def _qkv_rope(x, wqkv, cos_t, sa_t, sb_t, bsz, seq, q_scale):
    t, d = x.shape
    tm = min(ROW_TILE, seq)
    tiles_per_seq = seq // tm
    dv = d // LANES * V_ROWS
    row = pl.BlockSpec((tm, d), lambda i: (i, 0))
    col = pl.BlockSpec((1, d, tm), lambda i: (i // tiles_per_seq, 0, i % tiles_per_seq))
    col_v = pl.BlockSpec((1, dv, tm), lambda i: (i // tiles_per_seq, 0, i % tiles_per_seq))
    tab = pl.BlockSpec((tm, LANES), lambda i: (i % tiles_per_seq, 0))
    return pl.pallas_call(
        functools.partial(_qkv_rope_kernel, q_scale=q_scale),
        grid=(t // tm,),
        in_specs=[row, _const_spec(wqkv.shape), tab, tab, tab],
        out_specs=[col, row, col_v],
        out_shape=[jax.ShapeDtypeStruct((bsz, d, seq), BF16), jax.ShapeDtypeStruct((t, d), BF16),
                   jax.ShapeDtypeStruct((bsz, dv, seq), BF16)],
        compiler_params=_params(1),
        name="qkv_rope",
    )(x, wqkv, cos_t, sa_t, sb_t)


def _attn_kernel(lam_ref, g_ref, qt_ref, k_ref, vt_ref, o_ref, m_sc, acc_sc, *,
                 tq, tk, n_h, lambda_init):
    qi = pl.program_id(2)
    hd = LANES // 2
    cw = min(2 * LANES, tq)
    feat =lax.broadcasted_iota(jnp.int32, (LANES, tq), 0)
    qts = []
    for hh in range(n_h):
        qt = qt_ref[0, hh * LANES:(hh + 1) * LANES, :]
        zero = jnp.zeros_like(qt)
        qts.append(jnp.concatenate([jnp.where(feat < hd, qt, zero), jnp.where(feat >= hd, qt, zero)],
                                   axis=1))
    m_sc[...] = jnp.full(m_sc.shape, NEG_BIG, F32)
    acc_sc[...] = jnp.zeros(acc_sc.shape, F32)

    def step(start, width, masked):
        if masked:
            k_pos = start + lax.broadcasted_iota(jnp.int32, (width, tq), 0)
            q_pos = qi * tq + lax.broadcasted_iota(jnp.int32, (width, tq), 1)
            keep = k_pos <= q_pos
        scores = []
        for hh in range(n_h):
            kb = k_ref[0, pl.ds(start, width), hh * LANES:(hh + 1) * LANES]
            scores.append(_dot(kb, qts[hh]))
        for hh in range(n_h):
            vtb = vt_ref[0, hh * V_ROWS:(hh + 1) * V_ROWS, pl.ds(start, width)]
            for c, q0 in [(c, q0) for c in range(2) for q0 in range(0, tq, cw)]:
                i = 2 * hh + c
                qs_ = slice(q0, q0 + cw)
                s = scores[hh][:, c * tq + q0:c * tq + q0 + cw]
                if masked:
                    s = jnp.where(keep[:, qs_], s, NEG_BIG)
                m_prev = m_sc[i, :, qs_]
                m_new = jnp.maximum(m_prev, jnp.max(s, axis=0, keepdims=True))
                a = jnp.exp2(m_prev - m_new)
                p = jnp.exp2(s - m_new).astype(BF16)
                acc_sc[i, :, qs_] = a * acc_sc[i, :, qs_] + _dot(vtb, p)
                m_sc[i, :, qs_] = m_new

    n_wide = (qi * tq) // tk
    n_left = qi - n_wide * (tk // tq)

    def wide(j, carry):
        step(pl.multiple_of(j * tk, tk), tk, False)
        return carry

    def left(j, carry):
        step(pl.multiple_of(n_wide * tk + j * tq, tq), tq, False)
        return carry

    lax.fori_loop(0, n_wide, wide, 0)
    lax.fori_loop(0, n_left, left, 0)
    step(pl.multiple_of(qi * tq, tq), tq, True)

    lam = lam_ref[...]
    lam_full = (jnp.exp(jnp.sum(lam[0:1] * lam[1:2], axis=-1, keepdims=True))
                - jnp.exp(jnp.sum(lam[2:3] * lam[3:4], axis=-1, keepdims=True))
                + lambda_init)
    for hh in range(n_h):
        num = [acc_sc[2 * hh + c, 0:LANES, :] for c in range(2)]
        inv = [1.0 / acc_sc[2 * hh + c, LANES:LANES + 1, :] for c in range(2)]
        ot = num[0] * inv[0] - lam_full * (num[1] * inv[1])
        ot = ot * lax.rsqrt(jnp.mean(ot * ot, axis=0, keepdims=True) + LN_EPS) * g_ref[...]
        o_ref[0, :, hh * LANES:(hh + 1) * LANES] = (ot * (1.0 - lambda_init)).T.astype(BF16)


def _diff_attention(qt, k, vt, lam, subln_g, lambda_init):
    b, s, d = k.shape
    tq = min(Q_TILE, s)
    tk = max(tq, min(KV_TILE, s))
    assert tk % tq == 0
    n_h = HEADS_PER_STEP
    gw = n_h * LANES
    assert d % gw == 0
    qt_spec = pl.BlockSpec((1, gw, tq), lambda bi, h, qi: (bi, h, qi))
    k_spec = pl.BlockSpec((1, s, gw), lambda bi, h, qi: (bi, 0, h))
    vt_spec = pl.BlockSpec((1, n_h * V_ROWS, s), lambda bi, h, qi: (bi, h, 0))
    o_spec = pl.BlockSpec((1, tq, gw), lambda bi, h, qi: (bi, qi, h))
    g_col = subln_g.reshape(LANES, 1)
    return pl.pallas_call(
        functools.partial(_attn_kernel, tq=tq, tk=tk, n_h=n_h, lambda_init=lambda_init),
        grid=(b, d // gw, s // tq),
        in_specs=[_const_spec(lam.shape), _const_spec(g_col.shape), qt_spec, k_spec, vt_spec],
        out_specs=o_spec,
        out_shape=jax.ShapeDtypeStruct((b, s, d), BF16),
        scratch_shapes=[pltpu.VMEM((2 * n_h, 1, tq), F32),
                        pltpu.VMEM((2 * n_h, V_ROWS, tq), F32)],
        compiler_params=_params(3),
        name="diff_attn",
    )(lam, g_col, qt, k, vt)


def _proj_res_ln_kernel(x_ref, a_ref, w_ref, g_ref, b_ref, o_ref, *, alpha):
    h = _dot(a_ref[...], w_ref[...])
    o_ref[...] = _layer_norm(alpha * x_ref[...] + h, g_ref[...], b_ref[...])


def _proj_res_ln(x, a, w, g, b, alpha):
    t, d = x.shape
    tm = min(ROW_TILE, t)
    row = pl.BlockSpec((tm, d), lambda i: (i, 0))
    a_spec = pl.BlockSpec((tm, a.shape[1]), lambda i: (i, 0))
    return pl.pallas_call(
        functools.partial(_proj_res_ln_kernel, alpha=alpha),
        grid=(t // tm,),
        in_specs=[row, a_spec, _const_spec(w.shape), _const_spec(g.shape), _const_spec(b.shape)],
        out_specs=row,
        out_shape=jax.ShapeDtypeStruct((t, d), F32),
        compiler_params=_params(1),
        name="attn_out_ln",
    )(x, a, w, g, b)


def _rope_lane_tables(seq):
    head_dim = LANES // 2
    rot = head_dim // 4
    half = rot // 2
    lane = jnp.arange(LANES, dtype=jnp.int32) % head_dim
    inv_freq = ROPE_THETA ** (-(2 * (lane % half)).astype(F32) / rot)
    inv_freq = jnp.where(lane < rot, inv_freq, 0.0)
    ang = jnp.arange(seq, dtype=F32)[:, None] * inv_freq[None, :]
    c, s = jnp.cos(ang), jnp.sin(ang)
    sa = jnp.where(lane < half, -s, 0.0)
    sb = jnp.where((lane >= half) & (lane < rot), s, 0.0)
    return c, sa, sb


def _ffn_kernel(x_ref, w13_ref, w2_ref, g_ref, b_ref, o_ref, *, alpha, fc):
    x = x_ref[...]
    xb = x.astype(BF16)
    d_ff = w2_ref.shape[0]
    acc = None
    for c0 in range(0, d_ff, fc):
        gate = _dot(xb, w13_ref[:, c0:c0 + fc])
        up = _dot(xb, w13_ref[:, d_ff + c0:d_ff + c0 + fc])
        act = (gate * jax.nn.sigmoid(gate) * up).astype(BF16)
        part = _dot(act, w2_ref[c0:c0 + fc, :])
        acc = part if acc is None else acc + part
    o_ref[...] = _layer_norm(alpha * x + acc, g_ref[...], b_ref[...])


def _ffn(x, w13, w2, g, b, alpha):
    t, d = x.shape
    tm = min(ROW_TILE, t)
    row = pl.BlockSpec((tm, d), lambda i: (i, 0))
    return pl.pallas_call(
        functools.partial(_ffn_kernel, alpha=alpha, fc=_ffn_chunk(w2.shape[0])),
        grid=(t // tm,),
        in_specs=[row, _const_spec(w13.shape), _const_spec(w2.shape),
                  _const_spec(g.shape), _const_spec(b.shape)],
        out_specs=row,
        out_shape=jax.ShapeDtypeStruct((t, d), F32),
        compiler_params=_params(1),
        name="ffn",
    )(x, w13, w2, g, b)


def _ffn_chunk(d_ff):
    for c in (256, 128):
        if d_ff % c == 0:
            return c
    return d_ff


def _pw1_glu_kernel(x_ref, w_ref, b_ref, o_ref):
    d = x_ref.shape[1]
    h = _dot(x_ref[...].astype(BF16), w_ref[...]) + b_ref[...]
    o_ref[...] = h[:, :d] * jax.nn.sigmoid(h[:, d:])


def _pw1_glu(x, w, b):
    t, d = x.shape
    tm = min(ROW_TILE, t)
    row = pl.BlockSpec((tm, d), lambda i: (i, 0))
    return pl.pallas_call(
        _pw1_glu_kernel,
        grid=(t // tm,),
        in_specs=[row, _const_spec(w.shape), _const_spec(b.shape)],
        out_specs=row,
        out_shape=jax.ShapeDtypeStruct((t, d), F32),
        compiler_params=_params(1),
        name="conv_pw1_glu",
    )(x, w, b)


def _conv_kernel(x_ref, h_ref, halo_ref, dww_ref, dwb_ref, lng_ref, lnb_ref, w_ref, wb_ref,
                 g_ref, b_ref, o_ref, buf, shift_ref, *, alpha, tiles_per_seq):
    tm = x_ref.shape[0]
    width = dww_ref.shape[0]
    first = (pl.program_id(0) % tiles_per_seq) == 0
    halo = halo_ref[...]
    buf[0:CONV_HALO, :] = jnp.where(first, jnp.zeros_like(halo), halo)
    buf[CONV_HALO:, :] = h_ref[...]
    acc = jnp.zeros(h_ref.shape, F32) + dwb_ref[...]
    sub = 8
    n_a = (width + sub - 1) // sub
    top = sub * (n_a - 1)
    assert top + sub <= CONV_HALO
    for b in range(sub):
        if b:
            shift_ref[...] = buf[pl.ds(CONV_HALO - top - b, tm + top), :]
            src, base = shift_ref, top
        else:
            src, base = buf, CONV_HALO
        for a in range(n_a):
            j = sub * a + b
            if j < width:
                acc = acc + (src[pl.ds(base - sub * a, tm), :]
                             * dww_ref[width - 1 - j:width - j, :])
    y = _layer_norm(acc, lng_ref[...], lnb_ref[...])
    y = (y * jax.nn.sigmoid(y)).astype(BF16)
    y = _dot(y, w_ref[...]) + wb_ref[...]
    o_ref[...] = _layer_norm(alpha * x_ref[...] + y, g_ref[...], b_ref[...])


def _conv_tail(x, h, dw_w, dw_b, ln_g, ln_b, pw2_w, pw2_b, g, b, alpha, seq):
    t, d = x.shape
    tm = min(ROW_TILE, seq)
    tiles_per_seq = seq // tm
    ratio = tm // CONV_HALO
    row = pl.BlockSpec((tm, d), lambda i: (i, 0))
    halo = pl.BlockSpec((CONV_HALO, d), lambda i: (jnp.maximum(i * ratio - 1, 0), 0))
    consts = [dw_w, dw_b, ln_g, ln_b, pw2_w, pw2_b, g, b]
    return pl.pallas_call(
        functools.partial(_conv_kernel, alpha=alpha, tiles_per_seq=tiles_per_seq),
        grid=(t // tm,),
        in_specs=[row, row, halo] + [_const_spec(c.shape) for c in consts],
        out_specs=row,
        out_shape=jax.ShapeDtypeStruct((t, d), F32),
        scratch_shapes=[pltpu.VMEM((CONV_HALO + tm, d), F32),
                        pltpu.VMEM((tm + 8 * ((dw_w.shape[0] + 7) // 8 - 1), d), F32)],
        compiler_params=_params(1),
        name="conv_tail",
    )(x, h, h, *consts)


def _pool_kernel(x_ref, halo_ref, w_ref, pb_ref, ps_ref, g_ref, b_ref, o_ref, buf, *,
                 alpha, tiles_per_seq):
    tm, d = x_ref.shape
    n_groups = w_ref.shape[0]
    gw = d // n_groups
    tile = pl.program_id(0) % tiles_per_seq
    x = x_ref[...]
    halo = halo_ref[...]
    buf[0:POOL_HALO, :] = jnp.where(tile == 0, jnp.zeros_like(halo), halo)
    buf[POOL_HALO:, :] = x
    pos = tile * tm + lax.broadcasted_iota(jnp.int32, (tm, 1), 0)
    ys = []
    for gi, win in enumerate(POOL_WINDOWS):
        cols = slice(gi * gw, (gi + 1) * gw)
        tot = x[:, cols]
        for j in range(1, win):
            tot = tot + buf[pl.ds(POOL_HALO - j, tm), cols]
        count = jnp.minimum(pos + 1, win).astype(F32)
        pooled = (tot / count - x[:, cols]).astype(BF16)
        ys.append(_dot(pooled, w_ref[gi]))
    y = (jnp.concatenate(ys, axis=1) + pb_ref[...]) * ps_ref[...]
    o_ref[...] = _layer_norm(alpha * x + y, g_ref[...], b_ref[...])


def _pool_layer(x, pool_w, pool_b, pool_scale, g, b, alpha, seq):
    t, d = x.shape
    tm = min(ROW_TILE, seq)
    tiles_per_seq = seq // tm
    ratio = tm // POOL_HALO
    row = pl.BlockSpec((tm, d), lambda i: (i, 0))
    halo = pl.BlockSpec((POOL_HALO, d), lambda i: (jnp.maximum(i * ratio - 1, 0), 0))
    consts = [pool_w, pool_b, pool_scale, g, b]
    return pl.pallas_call(
        functools.partial(_pool_kernel, alpha=alpha, tiles_per_seq=tiles_per_seq),
        grid=(t // tm,),
        in_specs=[row, halo] + [_const_spec(c.shape) for c in consts],
        out_specs=row,
        out_shape=jax.ShapeDtypeStruct((t, d), F32),
        scratch_shapes=[pltpu.VMEM((POOL_HALO + tm, d), F32)],
        compiler_params=_params(1),
        name="pool_layer",
    )(x, x, *consts)


def kernel(x, attn_wqkv, attn_wo, attn_lambda, attn_subln_g, conv_pw1_w, conv_pw1_b, conv_dw_w, conv_dw_b, conv_ln_g, conv_ln_b, conv_pw2_w, conv_pw2_b, pool_w, pool_b, pool_scale, ffn_w13, ffn_w2, ln1_g, ln1_b, ln2_g, ln2_b):
    bsz, seq, d = x.shape
    depth = ffn_w13.shape[0]
    d_ff = ffn_w2.shape[1]
    alpha = float((2 * depth) ** 0.25)
    head_dim = d // N_HEADS // 2
    assert 2 * head_dim == LANES and seq % min(ROW_TILE, seq) == 0
    row2 = lambda v: v.reshape(1, -1)

    cos_t, sa_t, sb_t = _rope_lane_tables(seq)

    h = x.reshape(bsz * seq, d)
    for i in range(depth):
        mixer, j = i % N_MIXERS, i // N_MIXERS
        g1, b1 = row2(ln1_g[i]), row2(ln1_b[i])
        if mixer == 0:
            lambda_init = 0.8 - 0.6 * math.exp(-0.3 * i)
            qt, k, vt = _qkv_rope(h, attn_wqkv[j].astype(BF16), cos_t, sa_t, sb_t, bsz, seq,
                                  head_dim ** -0.5 * math.log2(math.e))
            o = _diff_attention(qt, k.reshape(bsz, seq, d), vt,
                                attn_lambda[j], attn_subln_g[j], lambda_init)
            h = _proj_res_ln(h, o.reshape(bsz * seq, d), attn_wo[j].astype(BF16), g1, b1, alpha)
        elif mixer == 1:
            glu = _pw1_glu(h, conv_pw1_w[j].astype(BF16), row2(conv_pw1_b[j]))
            h = _conv_tail(h, glu, conv_dw_w[j], row2(conv_dw_b[j]), row2(conv_ln_g[j]),
                           row2(conv_ln_b[j]), conv_pw2_w[j].astype(BF16), row2(conv_pw2_b[j]),
                           g1, b1, alpha, seq)
        else:
            h = _pool_layer(h, pool_w[j].astype(BF16), row2(pool_b[j]), row2(pool_scale[j]),
                            g1, b1, alpha, seq)
        h = _ffn(h, ffn_w13[i].astype(BF16), ffn_w2[i].astype(BF16),
                 row2(ln2_g[i]), row2(ln2_b[i]), alpha)
    return h.reshape(bsz, seq, d)
```

```python
import functools
import math

import jax
import jax.numpy as jnp
from jax import lax
from jax.experimental import pallas as pl
from jax.experimental.pallas import tpu as pltpu

F32 = jnp.float32
BF16 = jnp.bfloat16

N_MIXERS = 3
N_HEADS = 8
ROPE_THETA = 500000.0
POOL_WINDOWS = (2, 4, 8, 16)
LN_EPS = 1e-5
NEG_BIG = -1e30
HUGE = 3e38
SAFE_EXP = 60.0
F32_MAX_EXP = 127

LANES = 128
VMEM_LIMIT = 56 * 1024 * 1024
ROW_TILE = 512
Q_TILE = 512
KV_TILE = 1024
HEADS_PER_STEP = 2
SUM_ROWS = 16
V_ROWS = LANES + SUM_ROWS
CONV_HALO = 32
POOL_HALO = 16


def _params(n_axes, flags=None):
    return pltpu.CompilerParams(dimension_semantics=("arbitrary",) * n_axes,
                                vmem_limit_bytes=VMEM_LIMIT, flags=flags)


def _const_spec(shape):
    nd = len(shape)
    return pl.BlockSpec(shape, lambda *_: (0,) * nd, pipeline_mode=pl.Buffered(1))


def _dot(a, b):
    return jnp.dot(a, b, preferred_element_type=F32)


def _layer_norm(y, g, b):
    mu = jnp.mean(y, axis=-1, keepdims=True)
    yc = y - mu
    var = jnp.mean(yc * yc, axis=-1, keepdims=True)
    return yc * lax.rsqrt(var + LN_EPS) * g + b


def _qkv_rope_kernel(x_ref, w_ref, cos_ref, sa_ref, sb_ref, qt_ref, k_ref, vt_ref, kmax_ref, *,
                     q_scale):
    d = x_ref.shape[1]
    half_rot = 8
    xb = x_ref[...].astype(BF16)
    c = cos_ref[...]
    sa = sa_ref[...]
    sb = sb_ref[...]

    def rope(y):
        return (y * c + pltpu.roll(y, LANES - half_rot, 1) * sa
                + pltpu.roll(y, half_rot, 1) * sb)

    yq = _dot(xb, w_ref[:, 0:d])
    for h in range(d // LANES):
        sl = slice(h * LANES, (h + 1) * LANES)
        qt_ref[0, sl, :] = (rope(yq[:, sl]) * q_scale).T.astype(BF16)
    yk = _dot(xb, w_ref[:, d:2 * d])
    norms = []
    for h in range(d // LANES):
        sl = slice(h * LANES, (h + 1) * LANES)
        kh = rope(yk[:, sl]).astype(BF16)
        k_ref[:, sl] = kh
        k32 = kh.astype(F32)
        sq = jnp.sum(k32 * k32, axis=-1, keepdims=True)
        norms.append(jnp.broadcast_to(jnp.sqrt(jnp.max(sq, axis=0, keepdims=True)), (1, LANES)))
    yv = _dot(xb, w_ref[:, 2 * d:3 * d])
    ones = jnp.ones((SUM_ROWS, x_ref.shape[0]), BF16)
    for h in range(d // LANES):
        r0 = h * V_ROWS
        vh = yv[:, h * LANES:(h + 1) * LANES]
        vt_ref[0, r0:r0 + LANES, :] = vh.T.astype(BF16)
        vt_ref[0, r0 + LANES:r0 + V_ROWS, :] = ones
        vmax = jnp.max(jnp.max(jnp.abs(vh), axis=-1, keepdims=True), axis=0, keepdims=True)
        norms.append(jnp.broadcast_to(vmax, (1, LANES)))
    kmax_ref[0] = jnp.concatenate(norms, axis=0)


def _qkv_rope(x, wqkv, cos_t, sa_t, sb_t, bsz, seq, q_scale):
    t, d = x.shape
    tm = min(Q_TILE, seq)
    tiles_per_seq = seq // tm
    n_heads = d // LANES
    dv = n_heads * V_ROWS
    row = pl.BlockSpec((tm, d), lambda i: (i, 0))
    col = pl.BlockSpec((1, d, tm), lambda i: (i // tiles_per_seq, 0, i % tiles_per_seq))
    col_v = pl.BlockSpec((1, dv, tm), lambda i: (i // tiles_per_seq, 0, i % tiles_per_seq))
    tab = pl.BlockSpec((tm, LANES), lambda i: (i % tiles_per_seq, 0))
    kmx = pl.BlockSpec((1, 2 * n_heads, LANES), lambda i: (i, 0, 0))
    return pl.pallas_call(
        functools.partial(_qkv_rope_kernel, q_scale=q_scale),
        grid=(t // tm,),
        in_specs=[row, _const_spec(wqkv.shape), tab, tab, tab],
        out_specs=[col, row, col_v, kmx],
        out_shape=[jax.ShapeDtypeStruct((bsz, d, seq), BF16), jax.ShapeDtypeStruct((t, d), BF16),
                   jax.ShapeDtypeStruct((bsz, dv, seq), BF16),
                   jax.ShapeDtypeStruct((t // tm, 2 * n_heads, LANES), F32)],
        compiler_params=_params(1),
        name="qkv_rope",
    )(x, wqkv, cos_t, sa_t, sb_t)


def _attn_kernel(kmax_ref, lam_ref, g_ref, qt_ref, k_ref, vt_ref, o_ref, m_sc, acc_sc, *,
                 tq, tk, n_h, n_heads, v_ok, lambda_init):
    bi, hg, qi = pl.program_id(0), pl.program_id(1), pl.program_id(2)
    hd = LANES // 2
    cw = min(2 * LANES, tq)
    feat = lax.broadcasted_iota(jnp.int32, (LANES, tq), 0)
    qts, qnorm = [], []
    for hh in range(n_h):
        qt = qt_ref[0, hh * LANES:(hh + 1) * LANES, :]
        zero = jnp.zeros_like(qt)
        qts.append(jnp.concatenate([jnp.where(feat < hd, qt, zero), jnp.where(feat >= hd, qt, zero)],
                                   axis=1))
        q32 = qt.astype(F32)
        qnorm.append(jnp.sqrt(jnp.sum(q32 * q32, axis=0, keepdims=True)))
    m_sc[...] = jnp.full(m_sc.shape, NEG_BIG, F32)
    acc_sc[...] = jnp.zeros(acc_sc.shape, F32)

    def tile_max(start, width, row):
        base = (bi * (k_ref.shape[1] // tq) + start // tq) * 2 * n_heads + row
        out = kmax_ref[base]
        for t in range(1, width // tq):
            out = jnp.maximum(out, kmax_ref[base + t * 2 * n_heads])
        return out

    def norm_budget():
        out = []
        for hh in range(n_h):
            room = jnp.minimum(m_sc[2 * hh], m_sc[2 * hh + 1]) + SAFE_EXP
            ok = qnorm[hh] > 0.0
            ratio = jnp.where(ok, room / jnp.where(ok, qnorm[hh], 1.0),
                              jnp.where(room >= 0.0, HUGE, -HUGE))
            out.append(jnp.min(ratio))
        return tuple(out)

    def step(start, width, masked, rescale):
        if masked:
            k_pos = start + lax.broadcasted_iota(jnp.int32, (width, tq), 0)
            q_pos = qi * tq + lax.broadcasted_iota(jnp.int32, (width, tq), 1)
            keep = k_pos <= q_pos
        scores = []
        for hh in range(n_h):
            kb = k_ref[0, pl.ds(start, width), hh * LANES:(hh + 1) * LANES]
            scores.append(_dot(kb, qts[hh]))
        for hh in range(n_h):
            vtb = vt_ref[0, hh * V_ROWS:(hh + 1) * V_ROWS, pl.ds(start, width)]
            for c, q0 in [(c, q0) for c in range(2) for q0 in range(0, tq, cw)]:
                i = 2 * hh + c
                qs_ = slice(q0, q0 + cw)
                s = scores[hh][:, c * tq + q0:c * tq + q0 + cw]
                if masked:
                    s = jnp.where(keep[:, qs_], s, NEG_BIG)
                m_prev = m_sc[i, :, qs_]
                if not rescale:
                    p = jnp.exp2(s - m_prev).astype(BF16)
                    acc_sc[i, :, qs_] = acc_sc[i, :, qs_] + _dot(vtb, p)
                    continue
                m_new = jnp.maximum(m_prev, jnp.max(s, axis=0, keepdims=True))
                a = jnp.exp2(m_prev - m_new)
                p = jnp.exp2(s - m_new).astype(BF16)
                acc_sc[i, :, qs_] = a * acc_sc[i, :, qs_] + _dot(vtb, p)
                m_sc[i, :, qs_] = m_new

    n_wide = (qi * tq) // tk
    n_left = qi - n_wide * (tk // tq)

    def fold(start, width, masked, budget):
        safe = None
        for hh in range(n_h):
            head = hg * n_h + hh
            ok = jnp.logical_and(tile_max(start, width, head) <= budget[hh],
                                 tile_max(start, width, n_heads + head) <= v_ok)
            safe = ok if safe is None else jnp.logical_and(safe, ok)

        def keep_offsets():
            step(start, width, masked, False)
            return budget

        def move_offsets():
            step(start, width, masked, True)
            return norm_budget()

        return lax.cond(safe, keep_offsets, move_offsets)

    def wide(j, budget):
        return fold(pl.multiple_of(j * tk, tk), tk, False, budget)

    def left(j, budget):
        return fold(pl.multiple_of(n_wide * tk + j * tq, tq), tq, False, budget)

    budget = (jnp.float32(-HUGE),) * n_h
    budget = lax.fori_loop(0, n_wide, wide, budget)
    budget = lax.fori_loop(0, n_left, left, budget)
    fold(pl.multiple_of(qi * tq, tq), tq, True, budget)

    lam = lam_ref[...]
    lam_full = (jnp.exp(jnp.sum(lam[0:1] * lam[1:2], axis=-1, keepdims=True))
                - jnp.exp(jnp.sum(lam[2:3] * lam[3:4], axis=-1, keepdims=True))
                + lambda_init)
    for hh in range(n_h):
        num = [acc_sc[2 * hh + c, 0:LANES, :] for c in range(2)]
        inv = [1.0 / acc_sc[2 * hh + c, LANES:LANES + 1, :] for c in range(2)]
        ot = num[0] * inv[0] - lam_full * (num[1] * inv[1])
        ot = ot * lax.rsqrt(jnp.mean(ot * ot, axis=0, keepdims=True) + LN_EPS) * g_ref[...]
        o_ref[0, :, hh * LANES:(hh + 1) * LANES] = (ot * (1.0 - lambda_init)).T.astype(BF16)


def _diff_attention(kmax, qt, k, vt, lam, subln_g, lambda_init):
    b, s, d = k.shape
    tq = min(Q_TILE, s)
    tk = max(tq, min(KV_TILE, s))
    assert tk % tq == 0
    n_h = HEADS_PER_STEP
    gw = n_h * LANES
    assert d % gw == 0
    qt_spec = pl.BlockSpec((1, gw, tq), lambda bi, h, qi, _: (bi, h, qi))
    k_spec = pl.BlockSpec((1, s, gw), lambda bi, h, qi, _: (bi, 0, h))
    vt_spec = pl.BlockSpec((1, n_h * V_ROWS, s), lambda bi, h, qi, _: (bi, h, 0))
    o_spec = pl.BlockSpec((1, tq, gw), lambda bi, h, qi, _: (bi, qi, h))
    g_col = subln_g.reshape(LANES, 1)
    grid_spec = pltpu.PrefetchScalarGridSpec(
        num_scalar_prefetch=1,
        grid=(b, d // gw, s // tq),
        in_specs=[_const_spec(lam.shape), _const_spec(g_col.shape), qt_spec, k_spec, vt_spec],
        out_specs=o_spec,
        scratch_shapes=[pltpu.VMEM((2 * n_h, 1, tq), F32),
                        pltpu.VMEM((2 * n_h, V_ROWS, tq), F32)])
    return pl.pallas_call(
        functools.partial(_attn_kernel, tq=tq, tk=tk, n_h=n_h, n_heads=d // LANES,
                          v_ok=2.0 ** (F32_MAX_EXP - 2 - SAFE_EXP) / s, lambda_init=lambda_init),
        grid_spec=grid_spec,
        out_shape=jax.ShapeDtypeStruct((b, s, d), BF16),
        compiler_params=_params(3),
        name="diff_attn",
    )(kmax, lam, g_col, qt, k, vt)


def _proj_res_ln_kernel(x_ref, a_ref, w_ref, g_ref, b_ref, o_ref, *, alpha):
    h = _dot(a_ref[...], w_ref[...])
    o_ref[...] = _layer_norm(alpha * x_ref[...] + h, g_ref[...], b_ref[...])


def _proj_res_ln(x, a, w, g, b, alpha):
    t, d = x.shape
    tm = min(ROW_TILE, t)
    row = pl.BlockSpec((tm, d), lambda i: (i, 0))
    a_spec = pl.BlockSpec((tm, a.shape[1]), lambda i: (i, 0))
    return pl.pallas_call(
        functools.partial(_proj_res_ln_kernel, alpha=alpha),
        grid=(t // tm,),
        in_specs=[row, a_spec, _const_spec(w.shape), _const_spec(g.shape), _const_spec(b.shape)],
        out_specs=row,
        out_shape=jax.ShapeDtypeStruct((t, d), F32),
        compiler_params=_params(1),
        name="attn_out_ln",
    )(x, a, w, g, b)


def _rope_lane_tables(seq):
    head_dim = LANES // 2
    rot = head_dim // 4
    half = rot // 2
    lane = jnp.arange(LANES, dtype=jnp.int32) % head_dim
    inv_freq = ROPE_THETA ** (-(2 * (lane % half)).astype(F32) / rot)
    inv_freq = jnp.where(lane < rot, inv_freq, 0.0)
    ang = jnp.arange(seq, dtype=F32)[:, None] * inv_freq[None, :]
    c, s = jnp.cos(ang), jnp.sin(ang)
    sa = jnp.where(lane < half, -s, 0.0)
    sb = jnp.where((lane >= half) & (lane < rot), s, 0.0)
    return c, sa, sb


def _ffn_kernel(x_ref, w13_ref, w2_ref, g_ref, b_ref, o_ref, *, alpha, fc):
    x = x_ref[...]
    xb = x.astype(BF16)
    d_ff = w2_ref.shape[0]
    acc = None
    for c0 in range(0, d_ff, fc):
        gate = _dot(xb, w13_ref[:, c0:c0 + fc])
        up = _dot(xb, w13_ref[:, d_ff + c0:d_ff + c0 + fc])
        act = (gate * jax.nn.sigmoid(gate) * up).astype(BF16)
        part = _dot(act, w2_ref[c0:c0 + fc, :])
        acc = part if acc is None else acc + part
    o_ref[...] = _layer_norm(alpha * x + acc, g_ref[...], b_ref[...])


def _ffn(x, w13, w2, g, b, alpha):
    t, d = x.shape
    tm = min(ROW_TILE, t)
    row = pl.BlockSpec((tm, d), lambda i: (i, 0))
    return pl.pallas_call(
        functools.partial(_ffn_kernel, alpha=alpha, fc=_ffn_chunk(w2.shape[0])),
        grid=(t // tm,),
        in_specs=[row, _const_spec(w13.shape), _const_spec(w2.shape),
                  _const_spec(g.shape), _const_spec(b.shape)],
        out_specs=row,
        out_shape=jax.ShapeDtypeStruct((t, d), F32),
        compiler_params=_params(1),
        name="ffn",
    )(x, w13, w2, g, b)


def _ffn_chunk(d_ff):
    for c in (256, 128):
        if d_ff % c == 0:
            return c
    return d_ff


def _pw1_glu_kernel(x_ref, w_ref, b_ref, o_ref):
    d = x_ref.shape[1]
    h = _dot(x_ref[...].astype(BF16), w_ref[...]) + b_ref[...]
    o_ref[...] = h[:, :d] * jax.nn.sigmoid(h[:, d:])


def _pw1_glu(x, w, b):
    t, d = x.shape
    tm = min(ROW_TILE, t)
    row = pl.BlockSpec((tm, d), lambda i: (i, 0))
    return pl.pallas_call(
        _pw1_glu_kernel,
        grid=(t // tm,),
        in_specs=[row, _const_spec(w.shape), _const_spec(b.shape)],
        out_specs=row,
        out_shape=jax.ShapeDtypeStruct((t, d), F32),
        compiler_params=_params(1),
        name="conv_pw1_glu",
    )(x, w, b)


def _conv_kernel(x_ref, h_ref, halo_ref, dww_ref, dwb_ref, lng_ref, lnb_ref, w_ref, wb_ref,
                 g_ref, b_ref, o_ref, buf, shift_ref, *, alpha, tiles_per_seq):
    tm = x_ref.shape[0]
    width = dww_ref.shape[0]
    first = (pl.program_id(0) % tiles_per_seq) == 0
    halo = halo_ref[...]
    buf[0:CONV_HALO, :] = jnp.where(first, jnp.zeros_like(halo), halo)
    buf[CONV_HALO:, :] = h_ref[...]
    acc = jnp.zeros(h_ref.shape, F32) + dwb_ref[...]
    sub = 8
    n_a = (width + sub - 1) // sub
    top = sub * (n_a - 1)
    assert top + sub <= CONV_HALO
    for b in range(sub):
        if b:
            shift_ref[...] = buf[pl.ds(CONV_HALO - top - b, tm + top), :]
            src, base = shift_ref, top
        else:
            src, base = buf, CONV_HALO
        for a in range(n_a):
            j = sub * a + b
            if j < width:
                acc = acc + (src[pl.ds(base - sub * a, tm), :]
                             * dww_ref[width - 1 - j:width - j, :])
    y = _layer_norm(acc, lng_ref[...], lnb_ref[...])
    y = (y * jax.nn.sigmoid(y)).astype(BF16)
    y = _dot(y, w_ref[...]) + wb_ref[...]
    o_ref[...] = _layer_norm(alpha * x_ref[...] + y, g_ref[...], b_ref[...])


def _conv_tail(x, h, dw_w, dw_b, ln_g, ln_b, pw2_w, pw2_b, g, b, alpha, seq):
    t, d = x.shape
    tm = min(ROW_TILE, seq)
    tiles_per_seq = seq // tm
    ratio = tm // CONV_HALO
    row = pl.BlockSpec((tm, d), lambda i: (i, 0))
    halo = pl.BlockSpec((CONV_HALO, d), lambda i: (jnp.maximum(i * ratio - 1, 0), 0))
    consts = [dw_w, dw_b, ln_g, ln_b, pw2_w, pw2_b, g, b]
    return pl.pallas_call(
        functools.partial(_conv_kernel, alpha=alpha, tiles_per_seq=tiles_per_seq),
        grid=(t // tm,),
        in_specs=[row, row, halo] + [_const_spec(c.shape) for c in consts],
        out_specs=row,
        out_shape=jax.ShapeDtypeStruct((t, d), F32),
        scratch_shapes=[pltpu.VMEM((CONV_HALO + tm, d), F32),
                        pltpu.VMEM((tm + 8 * ((dw_w.shape[0] + 7) // 8 - 1), d), F32)],
        compiler_params=_params(1),
        name="conv_tail",
    )(x, h, h, *consts)


def _pool_kernel(x_ref, halo_ref, w_ref, pb_ref, ps_ref, g_ref, b_ref, o_ref, buf, *,
                 alpha, tiles_per_seq):
    tm, d = x_ref.shape
    n_groups = w_ref.shape[0]
    gw = d // n_groups
    tile = pl.program_id(0) % tiles_per_seq
    x = x_ref[...]
    halo = halo_ref[...]
    buf[0:POOL_HALO, :] = jnp.where(tile == 0, jnp.zeros_like(halo), halo)
    buf[POOL_HALO:, :] = x
    pos = tile * tm + lax.broadcasted_iota(jnp.int32, (tm, 1), 0)
    ys = []
    for gi, win in enumerate(POOL_WINDOWS):
        cols = slice(gi * gw, (gi + 1) * gw)
        tot = x[:, cols]
        for j in range(1, win):
            tot = tot + buf[pl.ds(POOL_HALO - j, tm), cols]
        count = jnp.minimum(pos + 1, win).astype(F32)
        pooled = (tot / count - x[:, cols]).astype(BF16)
        ys.append(_dot(pooled, w_ref[gi]))
    y = (jnp.concatenate(ys, axis=1) + pb_ref[...]) * ps_ref[...]
    o_ref[...] = _layer_norm(alpha * x + y, g_ref[...], b_ref[...])


def _pool_layer(x, pool_w, pool_b, pool_scale, g, b, alpha, seq):
    t, d = x.shape
    tm = min(ROW_TILE, seq)
    tiles_per_seq = seq // tm
    ratio = tm // POOL_HALO
    row = pl.BlockSpec((tm, d), lambda i: (i, 0))
    halo = pl.BlockSpec((POOL_HALO, d), lambda i: (jnp.maximum(i * ratio - 1, 0), 0))
    consts = [pool_w, pool_b, pool_scale, g, b]
    return pl.pallas_call(
        functools.partial(_pool_kernel, alpha=alpha, tiles_per_seq=tiles_per_seq),
        grid=(t // tm,),
        in_specs=[row, halo] + [_const_spec(c.shape) for c in consts],
        out_specs=row,
        out_shape=jax.ShapeDtypeStruct((t, d), F32),
        scratch_shapes=[pltpu.VMEM((POOL_HALO + tm, d), F32)],
        compiler_params=_params(1),
        name="pool_layer",
    )(x, x, *consts)


def kernel(x, attn_wqkv, attn_wo, attn_lambda, attn_subln_g, conv_pw1_w, conv_pw1_b, conv_dw_w, conv_dw_b, conv_ln_g, conv_ln_b, conv_pw2_w, conv_pw2_b, pool_w, pool_b, pool_scale, ffn_w13, ffn_w2, ln1_g, ln1_b, ln2_g, ln2_b):
    bsz, seq, d = x.shape
    depth = ffn_w13.shape[0]
    d_ff = ffn_w2.shape[1]
    alpha = float((2 * depth) ** 0.25)
    head_dim = d // N_HEADS // 2
    assert 2 * head_dim == LANES and seq % min(ROW_TILE, seq) == 0
    row2 = lambda v: v.reshape(1, -1)

    cos_t, sa_t, sb_t = _rope_lane_tables(seq)

    h = x.reshape(bsz * seq, d)
    for i in range(depth):
        mixer, j = i % N_MIXERS, i // N_MIXERS
        g1, b1 = row2(ln1_g[i]), row2(ln1_b[i])
        if mixer == 0:
            lambda_init = 0.8 - 0.6 * math.exp(-0.3 * i)
            qt, k, vt, kmax = _qkv_rope(h, attn_wqkv[j].astype(BF16), cos_t, sa_t, sb_t, bsz, seq,
                                        head_dim ** -0.5 * math.log2(math.e))
            o = _diff_attention(kmax[:, :, 0].reshape(-1), qt, k.reshape(bsz, seq, d), vt,
                                attn_lambda[j], attn_subln_g[j], lambda_init)
            h = _proj_res_ln(h, o.reshape(bsz * seq, d), attn_wo[j].astype(BF16), g1, b1, alpha)
        elif mixer == 1:
            glu = _pw1_glu(h, conv_pw1_w[j].astype(BF16), row2(conv_pw1_b[j]))
            h = _conv_tail(h, glu, conv_dw_w[j], row2(conv_dw_b[j]), row2(conv_ln_g[j]),
                           row2(conv_ln_b[j]), conv_pw2_w[j].astype(BF16), row2(conv_pw2_b[j]),
                           g1, b1, alpha, seq)
        else:
            h = _pool_layer(h, pool_w[j].astype(BF16), row2(pool_b[j]), row2(pool_scale[j]),
                            g1, b1, alpha, seq)
        h = _ffn(h, ffn_w13[i].astype(BF16), ffn_w2[i].astype(BF16),
                 row2(ln2_g[i]), row2(ln2_b[i]), alpha)
    return h.reshape(bsz, seq, d)
```

```python
import functools
import math

import jax
import jax.numpy as jnp
from jax import lax
from jax.experimental import pallas as pl
from jax.experimental.pallas import tpu as pltpu

F32 = jnp.float32
BF16 = jnp.bfloat16

N_MIXERS = 3
N_HEADS = 8
ROPE_THETA = 500000.0
POOL_WINDOWS = (2, 4, 8, 16)
LN_EPS = 1e-5
NEG_BIG = -1e30
HUGE = 3e38
SAFE_EXP = 60.0
F32_MAX_EXP = 127
NORM_SLACK = 1.01

LANES = 128
VMEM_LIMIT = 56 * 1024 * 1024
ROW_TILE = 512
Q_TILE = 512
KV_TILE = 1024
HEADS_PER_STEP = 2
SUM_ROWS = 16
V_ROWS = LANES + SUM_ROWS
CONV_HALO = 32
POOL_HALO = 16


def _params(n_axes, flags=None):
    return pltpu.CompilerParams(dimension_semantics=("arbitrary",) * n_axes,
                                vmem_limit_bytes=VMEM_LIMIT, flags=flags)


def _const_spec(shape):
    nd = len(shape)
    return pl.BlockSpec(shape, lambda *_: (0,) * nd, pipeline_mode=pl.Buffered(1))


def _dot(a, b):
    return jnp.dot(a, b, preferred_element_type=F32)


def _layer_norm(y, g, b):
    mu = jnp.mean(y, axis=-1, keepdims=True)
    yc = y - mu
    var = jnp.mean(yc * yc, axis=-1, keepdims=True)
    return yc * lax.rsqrt(var + LN_EPS) * g + b


def _qkv_rope_kernel(x_ref, w_ref, cos_ref, sa_ref, sb_ref, qt_ref, k_ref, vt_ref, kmax_ref, *,
                     q_scale):
    d = x_ref.shape[1]
    half_rot = 8
    xb = x_ref[...].astype(BF16)
    c = cos_ref[...]
    sa = sa_ref[...]
    sb = sb_ref[...]

    def rope(y):
        return (y * c + pltpu.roll(y, LANES - half_rot, 1) * sa
                + pltpu.roll(y, half_rot, 1) * sb)

    yq = _dot(xb, w_ref[:, 0:d])
    for h in range(d // LANES):
        sl = slice(h * LANES, (h + 1) * LANES)
        qt_ref[0, sl, :] = (rope(yq[:, sl]) * q_scale).T.astype(BF16)
    yk = _dot(xb, w_ref[:, d:2 * d])
    ksq = []
    for h in range(d // LANES):
        sl = slice(h * LANES, (h + 1) * LANES)
        kh = rope(yk[:, sl]).astype(BF16)
        k_ref[:, sl] = kh
        k32 = kh.astype(F32)
        ksq.append((k32 * k32).astype(BF16))
    head_of_feat = lax.broadcasted_iota(jnp.int32, (d, LANES), 0) // LANES
    col_id = lax.broadcasted_iota(jnp.int32, (d, LANES), 1)
    indicator = jnp.where(head_of_feat == col_id, 1.0, 0.0).astype(BF16)
    nsq = _dot(jnp.concatenate(ksq, axis=1), indicator)
    k_row = jnp.sqrt(jnp.max(nsq, axis=0, keepdims=True)) * NORM_SLACK
    yv = _dot(xb, w_ref[:, 2 * d:3 * d])
    v_abs = jnp.max(jnp.abs(yv), axis=0, keepdims=True)
    lane_id = lax.broadcasted_iota(jnp.int32, (1, LANES), 1)
    v_row = jnp.zeros((1, LANES), F32)
    ones = jnp.ones((SUM_ROWS, x_ref.shape[0]), BF16)
    for h in range(d // LANES):
        r0 = h * V_ROWS
        vh = yv[:, h * LANES:(h + 1) * LANES]
        vt_ref[0, r0:r0 + LANES, :] = vh.T.astype(BF16)
        vt_ref[0, r0 + LANES:r0 + V_ROWS, :] = ones
        v_max = jnp.max(v_abs[:, h * LANES:(h + 1) * LANES], axis=-1, keepdims=True)
        v_row = jnp.where(lane_id == h, v_max, v_row)
    kmax_ref[0] = jnp.concatenate([k_row, v_row, jnp.zeros((6, LANES), F32)], axis=0)


def _qkv_rope(x, wqkv, cos_t, sa_t, sb_t, bsz, seq, q_scale):
    t, d = x.shape
    tm = min(Q_TILE, seq)
    tiles_per_seq = seq // tm
    n_heads = d // LANES
    dv = n_heads * V_ROWS
    row = pl.BlockSpec((tm, d), lambda i: (i, 0))
    col = pl.BlockSpec((1, d, tm), lambda i: (i // tiles_per_seq, 0, i % tiles_per_seq))
    col_v = pl.BlockSpec((1, dv, tm), lambda i: (i // tiles_per_seq, 0, i % tiles_per_seq))
    tab = pl.BlockSpec((tm, LANES), lambda i: (i % tiles_per_seq, 0))
    kmx = pl.BlockSpec((1, 8, LANES), lambda i: (i, 0, 0))
    return pl.pallas_call(
        functools.partial(_qkv_rope_kernel, q_scale=q_scale),
        grid=(t // tm,),
        in_specs=[row, _const_spec(wqkv.shape), tab, tab, tab],
        out_specs=[col, row, col_v, kmx],
        out_shape=[jax.ShapeDtypeStruct((bsz, d, seq), BF16), jax.ShapeDtypeStruct((t, d), BF16),
                   jax.ShapeDtypeStruct((bsz, dv, seq), BF16),
                   jax.ShapeDtypeStruct((t // tm, 8, LANES), F32)],
        compiler_params=_params(1),
        name="qkv_rope",
    )(x, wqkv, cos_t, sa_t, sb_t)


def _attn_kernel(kmax_ref, lam_ref, g_ref, qt_ref, k_ref, vt_ref, o_ref, m_sc, acc_sc, *,
                 tq, tk, n_h, n_heads, v_ok, lambda_init):
    bi, hg, qi = pl.program_id(0), pl.program_id(1), pl.program_id(2)
    hd = LANES // 2
    cw = min(2 * LANES, tq)
    feat = lax.broadcasted_iota(jnp.int32, (LANES, tq), 0)
    qts, qnorm = [], []
    for hh in range(n_h):
        qt = qt_ref[0, hh * LANES:(hh + 1) * LANES, :]
        zero = jnp.zeros_like(qt)
        qts.append(jnp.concatenate([jnp.where(feat < hd, qt, zero), jnp.where(feat >= hd, qt, zero)],
                                   axis=1))
        q32 = qt.astype(F32)
        qnorm.append(jnp.sqrt(jnp.sum(q32 * q32, axis=0, keepdims=True)))
    m_sc[...] = jnp.full(m_sc.shape, NEG_BIG, F32)
    acc_sc[...] = jnp.zeros(acc_sc.shape, F32)

    def tile_max(start, width, row):
        base = (bi * (k_ref.shape[1] // tq) + start // tq) * 2 * n_heads + row
        out = kmax_ref[base]
        for t in range(1, width // tq):
            out = jnp.maximum(out, kmax_ref[base + t * 2 * n_heads])
        return out

    def norm_budget():
        out = []
        for hh in range(n_h):
            room = jnp.minimum(m_sc[2 * hh], m_sc[2 * hh + 1]) + SAFE_EXP
            ok = qnorm[hh] > 0.0
            ratio = jnp.where(ok, room / jnp.where(ok, qnorm[hh], 1.0),
                              jnp.where(room >= 0.0, HUGE, -HUGE))
            out.append(jnp.min(ratio))
        return tuple(out)

    def step(start, width, masked, rescale):
        if masked:
            k_pos = start + lax.broadcasted_iota(jnp.int32, (width, tq), 0)
            q_pos = qi * tq + lax.broadcasted_iota(jnp.int32, (width, tq), 1)
            keep = k_pos <= q_pos
        scores = []
        for hh in range(n_h):
            kb = k_ref[0, pl.ds(start, width), hh * LANES:(hh + 1) * LANES]
            scores.append(_dot(kb, qts[hh]))
        for hh in range(n_h):
            vtb = vt_ref[0, hh * V_ROWS:(hh + 1) * V_ROWS, pl.ds(start, width)]
            for c, q0 in [(c, q0) for c in range(2) for q0 in range(0, tq, cw)]:
                i = 2 * hh + c
                qs_ = slice(q0, q0 + cw)
                s = scores[hh][:, c * tq + q0:c * tq + q0 + cw]
                if masked:
                    s = jnp.where(keep[:, qs_], s, NEG_BIG)
                m_prev = m_sc[i, :, qs_]
                if not rescale:
                    p = jnp.exp2(s - m_prev).astype(BF16)
                    acc_sc[i, :, qs_] = acc_sc[i, :, qs_] + _dot(vtb, p)
                    continue
                m_new = jnp.maximum(m_prev, jnp.max(s, axis=0, keepdims=True))
                a = jnp.exp2(m_prev - m_new)
                p = jnp.exp2(s - m_new).astype(BF16)
                acc_sc[i, :, qs_] = a * acc_sc[i, :, qs_] + _dot(vtb, p)
                m_sc[i, :, qs_] = m_new

    n_wide = (qi * tq) // tk
    n_left = qi - n_wide * (tk // tq)

    def fold(start, width, masked, budget):
        safe = None
        for hh in range(n_h):
            head = hg * n_h + hh
            ok = jnp.logical_and(tile_max(start, width, head) <= budget[hh],
                                 tile_max(start, width, n_heads + head) <= v_ok)
            safe = ok if safe is None else jnp.logical_and(safe, ok)

        def keep_offsets():
            step(start, width, masked, False)
            return budget

        def move_offsets():
            step(start, width, masked, True)
            return norm_budget()

        return lax.cond(safe, keep_offsets, move_offsets)

    def wide(j, budget):
        return fold(pl.multiple_of(j * tk, tk), tk, False, budget)

    def left(j, budget):
        return fold(pl.multiple_of(n_wide * tk + j * tq, tq), tq, False, budget)

    budget = (jnp.float32(-HUGE),) * n_h
    budget = lax.fori_loop(0, n_wide, wide, budget)
    budget = lax.fori_loop(0, n_left, left, budget)
    fold(pl.multiple_of(qi * tq, tq), tq, True, budget)

    lam = lam_ref[...]
    lam_full = (jnp.exp(jnp.sum(lam[0:1] * lam[1:2], axis=-1, keepdims=True))
                - jnp.exp(jnp.sum(lam[2:3] * lam[3:4], axis=-1, keepdims=True))
                + lambda_init)
    for hh in range(n_h):
        num = [acc_sc[2 * hh + c, 0:LANES, :] for c in range(2)]
        inv = [1.0 / acc_sc[2 * hh + c, LANES:LANES + 1, :] for c in range(2)]
        ot = num[0] * inv[0] - lam_full * (num[1] * inv[1])
        ot = ot * lax.rsqrt(jnp.mean(ot * ot, axis=0, keepdims=True) + LN_EPS) * g_ref[...]
        o_ref[0, :, hh * LANES:(hh + 1) * LANES] = (ot * (1.0 - lambda_init)).T.astype(BF16)


def _diff_attention(kmax, qt, k, vt, lam, subln_g, lambda_init):
    b, s, d = k.shape
    tq = min(Q_TILE, s)
    tk = max(tq, min(KV_TILE, s))
    assert tk % tq == 0
    n_h = HEADS_PER_STEP
    gw = n_h * LANES
    assert d % gw == 0
    qt_spec = pl.BlockSpec((1, gw, tq), lambda bi, h, qi, _: (bi, h, qi))
    k_spec = pl.BlockSpec((1, s, gw), lambda bi, h, qi, _: (bi, 0, h))
    vt_spec = pl.BlockSpec((1, n_h * V_ROWS, s), lambda bi, h, qi, _: (bi, h, 0))
    o_spec = pl.BlockSpec((1, tq, gw), lambda bi, h, qi, _: (bi, qi, h))
    g_col = subln_g.reshape(LANES, 1)
    grid_spec = pltpu.PrefetchScalarGridSpec(
        num_scalar_prefetch=1,
        grid=(b, d // gw, s // tq),
        in_specs=[_const_spec(lam.shape), _const_spec(g_col.shape), qt_spec, k_spec, vt_spec],
        out_specs=o_spec,
        scratch_shapes=[pltpu.VMEM((2 * n_h, 1, tq), F32),
                        pltpu.VMEM((2 * n_h, V_ROWS, tq), F32)])
    return pl.pallas_call(
        functools.partial(_attn_kernel, tq=tq, tk=tk, n_h=n_h, n_heads=d // LANES,
                          v_ok=2.0 ** (F32_MAX_EXP - 2 - SAFE_EXP) / s, lambda_init=lambda_init),
        grid_spec=grid_spec,
        out_shape=jax.ShapeDtypeStruct((b, s, d), BF16),
        compiler_params=_params(3),
        name="diff_attn",
    )(kmax, lam, g_col, qt, k, vt)


def _proj_res_ln_kernel(x_ref, a_ref, w_ref, g_ref, b_ref, o_ref, *, alpha):
    h = _dot(a_ref[...], w_ref[...])
    o_ref[...] = _layer_norm(alpha * x_ref[...] + h, g_ref[...], b_ref[...])


def _proj_res_ln(x, a, w, g, b, alpha):
    t, d = x.shape
    tm = min(ROW_TILE, t)
    row = pl.BlockSpec((tm, d), lambda i: (i, 0))
    a_spec = pl.BlockSpec((tm, a.shape[1]), lambda i: (i, 0))
    return pl.pallas_call(
        functools.partial(_proj_res_ln_kernel, alpha=alpha),
        grid=(t // tm,),
        in_specs=[row, a_spec, _const_spec(w.shape), _const_spec(g.shape), _const_spec(b.shape)],
        out_specs=row,
        out_shape=jax.ShapeDtypeStruct((t, d), F32),
        compiler_params=_params(1),
        name="attn_out_ln",
    )(x, a, w, g, b)


def _rope_lane_tables(seq):
    head_dim = LANES // 2
    rot = head_dim // 4
    half = rot // 2
    lane = jnp.arange(LANES, dtype=jnp.int32) % head_dim
    inv_freq = ROPE_THETA ** (-(2 * (lane % half)).astype(F32) / rot)
    inv_freq = jnp.where(lane < rot, inv_freq, 0.0)
    ang = jnp.arange(seq, dtype=F32)[:, None] * inv_freq[None, :]
    c, s = jnp.cos(ang), jnp.sin(ang)
    sa = jnp.where(lane < half, -s, 0.0)
    sb = jnp.where((lane >= half) & (lane < rot), s, 0.0)
    return c, sa, sb


def _ffn_kernel(x_ref, w13_ref, w2_ref, g_ref, b_ref, o_ref, *, alpha, fc):
    x = x_ref[...]
    xb = x.astype(BF16)
    d_ff = w2_ref.shape[0]
    acc = None
    for c0 in range(0, d_ff, fc):
        gate = _dot(xb, w13_ref[:, c0:c0 + fc])
        up = _dot(xb, w13_ref[:, d_ff + c0:d_ff + c0 + fc])
        act = (gate * jax.nn.sigmoid(gate) * up).astype(BF16)
        part = _dot(act, w2_ref[c0:c0 + fc, :])
        acc = part if acc is None else acc + part
    o_ref[...] = _layer_norm(alpha * x + acc, g_ref[...], b_ref[...])


def _ffn(x, w13, w2, g, b, alpha):
    t, d = x.shape
    tm = min(ROW_TILE, t)
    row = pl.BlockSpec((tm, d), lambda i: (i, 0))
    return pl.pallas_call(
        functools.partial(_ffn_kernel, alpha=alpha, fc=_ffn_chunk(w2.shape[0])),
        grid=(t // tm,),
        in_specs=[row, _const_spec(w13.shape), _const_spec(w2.shape),
                  _const_spec(g.shape), _const_spec(b.shape)],
        out_specs=row,
        out_shape=jax.ShapeDtypeStruct((t, d), F32),
        compiler_params=_params(1),
        name="ffn",
    )(x, w13, w2, g, b)


def _ffn_chunk(d_ff):
    for c in (256, 128):
        if d_ff % c == 0:
            return c
    return d_ff


def _pw1_glu_kernel(x_ref, w_ref, b_ref, o_ref):
    d = x_ref.shape[1]
    h = _dot(x_ref[...].astype(BF16), w_ref[...]) + b_ref[...]
    o_ref[...] = h[:, :d] * jax.nn.sigmoid(h[:, d:])


def _pw1_glu(x, w, b):
    t, d = x.shape
    tm = min(ROW_TILE, t)
    row = pl.BlockSpec((tm, d), lambda i: (i, 0))
    return pl.pallas_call(
        _pw1_glu_kernel,
        grid=(t // tm,),
        in_specs=[row, _const_spec(w.shape), _const_spec(b.shape)],
        out_specs=row,
        out_shape=jax.ShapeDtypeStruct((t, d), F32),
        compiler_params=_params(1),
        name="conv_pw1_glu",
    )(x, w, b)


def _conv_kernel(x_ref, h_ref, halo_ref, dww_ref, dwb_ref, lng_ref, lnb_ref, w_ref, wb_ref,
                 g_ref, b_ref, o_ref, buf, shift_ref, *, alpha, tiles_per_seq):
    tm = x_ref.shape[0]
    width = dww_ref.shape[0]
    first = (pl.program_id(0) % tiles_per_seq) == 0
    halo = halo_ref[...]
    buf[0:CONV_HALO, :] = jnp.where(first, jnp.zeros_like(halo), halo)
    buf[CONV_HALO:, :] = h_ref[...]
    acc = jnp.zeros(h_ref.shape, F32) + dwb_ref[...]
    sub = 8
    n_a = (width + sub - 1) // sub
    top = sub * (n_a - 1)
    assert top + sub <= CONV_HALO
    for b in range(sub):
        if b:
            shift_ref[...] = buf[pl.ds(CONV_HALO - top - b, tm + top), :]
            src, base = shift_ref, top
        else:
            src, base = buf, CONV_HALO
        for a in range(n_a):
            j = sub * a + b
            if j < width:
                acc = acc + (src[pl.ds(base - sub * a, tm), :]
                             * dww_ref[width - 1 - j:width - j, :])
    y = _layer_norm(acc, lng_ref[...], lnb_ref[...])
    y = (y * jax.nn.sigmoid(y)).astype(BF16)
    y = _dot(y, w_ref[...]) + wb_ref[...]
    o_ref[...] = _layer_norm(alpha * x_ref[...] + y, g_ref[...], b_ref[...])


def _conv_tail(x, h, dw_w, dw_b, ln_g, ln_b, pw2_w, pw2_b, g, b, alpha, seq):
    t, d = x.shape
    tm = min(ROW_TILE, seq)
    tiles_per_seq = seq // tm
    ratio = tm // CONV_HALO
    row = pl.BlockSpec((tm, d), lambda i: (i, 0))
    halo = pl.BlockSpec((CONV_HALO, d), lambda i: (jnp.maximum(i * ratio - 1, 0), 0))
    consts = [dw_w, dw_b, ln_g, ln_b, pw2_w, pw2_b, g, b]
    return pl.pallas_call(
        functools.partial(_conv_kernel, alpha=alpha, tiles_per_seq=tiles_per_seq),
        grid=(t // tm,),
        in_specs=[row, row, halo] + [_const_spec(c.shape) for c in consts],
        out_specs=row,
        out_shape=jax.ShapeDtypeStruct((t, d), F32),
        scratch_shapes=[pltpu.VMEM((CONV_HALO + tm, d), F32),
                        pltpu.VMEM((tm + 8 * ((dw_w.shape[0] + 7) // 8 - 1), d), F32)],
        compiler_params=_params(1),
        name="conv_tail",
    )(x, h, h, *consts)


def _pool_kernel(x_ref, halo_ref, w_ref, pb_ref, ps_ref, g_ref, b_ref, o_ref, buf, *,
                 alpha, tiles_per_seq):
    tm, d = x_ref.shape
    n_groups = w_ref.shape[0]
    gw = d // n_groups
    tile = pl.program_id(0) % tiles_per_seq
    x = x_ref[...]
    halo = halo_ref[...]
    buf[0:POOL_HALO, :] = jnp.where(tile == 0, jnp.zeros_like(halo), halo)
    buf[POOL_HALO:, :] = x
    pos = tile * tm + lax.broadcasted_iota(jnp.int32, (tm, 1), 0)
    ys = []
    for gi, win in enumerate(POOL_WINDOWS):
        cols = slice(gi * gw, (gi + 1) * gw)
        tot = x[:, cols]
        for j in range(1, win):
            tot = tot + buf[pl.ds(POOL_HALO - j, tm), cols]
        count = jnp.minimum(pos + 1, win).astype(F32)
        pooled = (tot / count - x[:, cols]).astype(BF16)
        ys.append(_dot(pooled, w_ref[gi]))
    y = (jnp.concatenate(ys, axis=1) + pb_ref[...]) * ps_ref[...]
    o_ref[...] = _layer_norm(alpha * x + y, g_ref[...], b_ref[...])


def _pool_layer(x, pool_w, pool_b, pool_scale, g, b, alpha, seq):
    t, d = x.shape
    tm = min(ROW_TILE, seq)
    tiles_per_seq = seq // tm
    ratio = tm // POOL_HALO
    row = pl.BlockSpec((tm, d), lambda i: (i, 0))
    halo = pl.BlockSpec((POOL_HALO, d), lambda i: (jnp.maximum(i * ratio - 1, 0), 0))
    consts = [pool_w, pool_b, pool_scale, g, b]
    return pl.pallas_call(
        functools.partial(_pool_kernel, alpha=alpha, tiles_per_seq=tiles_per_seq),
        grid=(t // tm,),
        in_specs=[row, halo] + [_const_spec(c.shape) for c in consts],
        out_specs=row,
        out_shape=jax.ShapeDtypeStruct((t, d), F32),
        scratch_shapes=[pltpu.VMEM((POOL_HALO + tm, d), F32)],
        compiler_params=_params(1),
        name="pool_layer",
    )(x, x, *consts)


def kernel(x, attn_wqkv, attn_wo, attn_lambda, attn_subln_g, conv_pw1_w, conv_pw1_b, conv_dw_w, conv_dw_b, conv_ln_g, conv_ln_b, conv_pw2_w, conv_pw2_b, pool_w, pool_b, pool_scale, ffn_w13, ffn_w2, ln1_g, ln1_b, ln2_g, ln2_b):
    bsz, seq, d = x.shape
    depth = ffn_w13.shape[0]
    d_ff = ffn_w2.shape[1]
    alpha = float((2 * depth) ** 0.25)
    head_dim = d // N_HEADS // 2
    assert 2 * head_dim == LANES and seq % min(ROW_TILE, seq) == 0
    row2 = lambda v: v.reshape(1, -1)

    cos_t, sa_t, sb_t = _rope_lane_tables(seq)

    h = x.reshape(bsz * seq, d)
    for i in range(depth):
        mixer, j = i % N_MIXERS, i // N_MIXERS
        g1, b1 = row2(ln1_g[i]), row2(ln1_b[i])
        if mixer == 0:
            lambda_init = 0.8 - 0.6 * math.exp(-0.3 * i)
            qt, k, vt, kmax = _qkv_rope(h, attn_wqkv[j].astype(BF16), cos_t, sa_t, sb_t, bsz, seq,
                                        head_dim ** -0.5 * math.log2(math.e))
            o = _diff_attention(kmax[:, :2, :d // LANES].reshape(-1), qt, k.reshape(bsz, seq, d), vt,
                                attn_lambda[j], attn_subln_g[j], lambda_init)
            h = _proj_res_ln(h, o.reshape(bsz * seq, d), attn_wo[j].astype(BF16), g1, b1, alpha)
        elif mixer == 1:
            glu = _pw1_glu(h, conv_pw1_w[j].astype(BF16), row2(conv_pw1_b[j]))
            h = _conv_tail(h, glu, conv_dw_w[j], row2(conv_dw_b[j]), row2(conv_ln_g[j]),
                           row2(conv_ln_b[j]), conv_pw2_w[j].astype(BF16), row2(conv_pw2_b[j]),
                           g1, b1, alpha, seq)
        else:
            h = _pool_layer(h, pool_w[j].astype(BF16), row2(pool_b[j]), row2(pool_scale[j]),
                            g1, b1, alpha, seq)
        h = _ffn(h, ffn_w13[i].astype(BF16), ffn_w2[i].astype(BF16),
                 row2(ln2_g[i]), row2(ln2_b[i]), alpha)
    return h.reshape(bsz, seq, d)
```

```python
import functools
import math

import jax
import jax.numpy as jnp
from jax import lax
from jax.experimental import pallas as pl
from jax.experimental.pallas import tpu as pltpu

F32 = jnp.float32
BF16 = jnp.bfloat16

N_MIXERS = 3
N_HEADS = 8
ROPE_THETA = 500000.0
POOL_WINDOWS = (2, 4, 8, 16)
LN_EPS = 1e-5
NEG_BIG = -1e30
HUGE = 3e38
SAFE_EXP = 60.0
F32_MAX_EXP = 127
NORM_SLACK = 1.01
INIT_BOUND_MAX = 90.0

LANES = 128
VMEM_LIMIT = 56 * 1024 * 1024
ROW_TILE = 512
FFN_STREAMS = 2
Q_TILE = 512
KV_TILE = 1024
HEADS_PER_STEP = 2
SUM_ROWS = 16
V_ROWS = LANES + SUM_ROWS
CONV_HALO = 32
POOL_HALO = 16


def _params(n_axes, flags=None):
    return pltpu.CompilerParams(dimension_semantics=("arbitrary",) * n_axes,
                                vmem_limit_bytes=VMEM_LIMIT, flags=flags)


def _const_spec(shape):
    nd = len(shape)
    return pl.BlockSpec(shape, lambda *_: (0,) * nd, pipeline_mode=pl.Buffered(1))


def _dot(a, b):
    return jnp.dot(a, b, preferred_element_type=F32)


def _layer_norm(y, g, b):
    mu = jnp.mean(y, axis=-1, keepdims=True)
    yc = y - mu
    var = jnp.mean(yc * yc, axis=-1, keepdims=True)
    return yc * lax.rsqrt(var + LN_EPS) * g + b


def _qkv_rope_kernel(x_ref, w_ref, cos_ref, sa_ref, sb_ref, qt_ref, k_ref, vt_ref, kmax_ref, *,
                     q_scale):
    d = x_ref.shape[1]
    half_rot = 8
    xb = x_ref[...].astype(BF16)
    c = cos_ref[...]
    sa = sa_ref[...]
    sb = sb_ref[...]

    def rope(y):
        return (y * c + pltpu.roll(y, LANES - half_rot, 1) * sa
                + pltpu.roll(y, half_rot, 1) * sb)

    yq = _dot(xb, w_ref[:, 0:d])
    for h in range(d // LANES):
        sl = slice(h * LANES, (h + 1) * LANES)
        qt_ref[0, sl, :] = (rope(yq[:, sl]) * q_scale).T.astype(BF16)
    yk = _dot(xb, w_ref[:, d:2 * d])
    ksq = []
    for h in range(d // LANES):
        sl = slice(h * LANES, (h + 1) * LANES)
        kh = rope(yk[:, sl]).astype(BF16)
        k_ref[:, sl] = kh
        k32 = kh.astype(F32)
        ksq.append((k32 * k32).astype(BF16))
    head_of_feat = lax.broadcasted_iota(jnp.int32, (d, LANES), 0) // LANES
    col_id = lax.broadcasted_iota(jnp.int32, (d, LANES), 1)
    indicator = jnp.where(head_of_feat == col_id, 1.0, 0.0).astype(BF16)
    nsq = _dot(jnp.concatenate(ksq, axis=1), indicator)
    k_row = jnp.sqrt(jnp.max(nsq, axis=0, keepdims=True)) * NORM_SLACK
    yv = _dot(xb, w_ref[:, 2 * d:3 * d])
    v_abs = jnp.max(jnp.abs(yv), axis=0, keepdims=True)
    lane_id = lax.broadcasted_iota(jnp.int32, (1, LANES), 1)
    v_row = jnp.zeros((1, LANES), F32)
    ones = jnp.ones((SUM_ROWS, x_ref.shape[0]), BF16)
    for h in range(d // LANES):
        r0 = h * V_ROWS
        vh = yv[:, h * LANES:(h + 1) * LANES]
        vt_ref[0, r0:r0 + LANES, :] = vh.T.astype(BF16)
        vt_ref[0, r0 + LANES:r0 + V_ROWS, :] = ones
        v_max = jnp.max(v_abs[:, h * LANES:(h + 1) * LANES], axis=-1, keepdims=True)
        v_row = jnp.where(lane_id == h, v_max, v_row)
    kmax_ref[0] = jnp.concatenate([k_row, v_row, jnp.zeros((6, LANES), F32)], axis=0)


def _qkv_rope(x, wqkv, cos_t, sa_t, sb_t, bsz, seq, q_scale):
    t, d = x.shape
    tm = min(Q_TILE, seq)
    tiles_per_seq = seq // tm
    n_heads = d // LANES
    dv = n_heads * V_ROWS
    row = pl.BlockSpec((tm, d), lambda i: (i, 0))
    col = pl.BlockSpec((1, d, tm), lambda i: (i // tiles_per_seq, 0, i % tiles_per_seq))
    col_v = pl.BlockSpec((1, dv, tm), lambda i: (i // tiles_per_seq, 0, i % tiles_per_seq))
    tab = pl.BlockSpec((tm, LANES), lambda i: (i % tiles_per_seq, 0))
    kmx = pl.BlockSpec((1, 8, LANES), lambda i: (i, 0, 0))
    return pl.pallas_call(
        functools.partial(_qkv_rope_kernel, q_scale=q_scale),
        grid=(t // tm,),
        in_specs=[row, _const_spec(wqkv.shape), tab, tab, tab],
        out_specs=[col, row, col_v, kmx],
        out_shape=[jax.ShapeDtypeStruct((bsz, d, seq), BF16), jax.ShapeDtypeStruct((t, d), BF16),
                   jax.ShapeDtypeStruct((bsz, dv, seq), BF16),
                   jax.ShapeDtypeStruct((t // tm, 8, LANES), F32)],
        compiler_params=_params(1),
        name="qkv_rope",
    )(x, wqkv, cos_t, sa_t, sb_t)


def _attn_kernel(kmax_ref, lam_ref, g_ref, qt_ref, k_ref, vt_ref, o_ref, m_sc, acc_sc, *,
                 tq, tk, n_h, n_heads, v_ok, lambda_init):
    bi, hg, qi = pl.program_id(0), pl.program_id(1), pl.program_id(2)
    hd = LANES // 2
    cw = min(2 * LANES, tq)
    feat = lax.broadcasted_iota(jnp.int32, (LANES, tq), 0)
    qts, qnorm = [], []
    for hh in range(n_h):
        qt = qt_ref[0, hh * LANES:(hh + 1) * LANES, :]
        zero = jnp.zeros_like(qt)
        qts.append(jnp.concatenate([jnp.where(feat < hd, qt, zero), jnp.where(feat >= hd, qt, zero)],
                                   axis=1))
        q32 = qt.astype(F32)
        qnorm.append(jnp.sqrt(jnp.sum(q32 * q32, axis=0, keepdims=True)))
    acc_sc[...] = jnp.zeros(acc_sc.shape, F32)

    n_tiles = pl.num_programs(0) * (k_ref.shape[1] // tq)
    k_all = [kmax_ref[n_tiles * 2 * n_heads + bi * n_heads + hg * n_h + hh] for hh in range(n_h)]
    start_high = None
    for hh in range(n_h):
        ok = k_all[hh] * jnp.max(qnorm[hh]) <= INIT_BOUND_MAX
        start_high = ok if start_high is None else jnp.logical_and(start_high, ok)
    for hh in range(n_h):
        m0 = jnp.where(start_high, k_all[hh] * qnorm[hh] - SAFE_EXP, NEG_BIG)
        m_sc[2 * hh] = m0
        m_sc[2 * hh + 1] = m0

    def tile_max(start, width, row):
        base = (bi * (k_ref.shape[1] // tq) + start // tq) * 2 * n_heads + row
        out = kmax_ref[base]
        for t in range(1, width // tq):
            out = jnp.maximum(out, kmax_ref[base + t * 2 * n_heads])
        return out

    def norm_budget():
        out = []
        for hh in range(n_h):
            room = jnp.minimum(m_sc[2 * hh], m_sc[2 * hh + 1]) + SAFE_EXP
            ok = qnorm[hh] > 0.0
            ratio = jnp.where(ok, room / jnp.where(ok, qnorm[hh], 1.0),
                              jnp.where(room >= 0.0, HUGE, -HUGE))
            out.append(jnp.min(ratio))
        return tuple(out)

    def step(start, width, masked, rescale):
        if masked:
            k_pos = start + lax.broadcasted_iota(jnp.int32, (width, tq), 0)
            q_pos = qi * tq + lax.broadcasted_iota(jnp.int32, (width, tq), 1)
            keep = k_pos <= q_pos
        scores = []
        for hh in range(n_h):
            kb = k_ref[0, pl.ds(start, width), hh * LANES:(hh + 1) * LANES]
            scores.append(_dot(kb, qts[hh]))
        for hh in range(n_h):
            vtb = vt_ref[0, hh * V_ROWS:(hh + 1) * V_ROWS, pl.ds(start, width)]
            for c, q0 in [(c, q0) for c in range(2) for q0 in range(0, tq, cw)]:
                i = 2 * hh + c
                qs_ = slice(q0, q0 + cw)
                s = scores[hh][:, c * tq + q0:c * tq + q0 + cw]
                if masked:
                    s = jnp.where(keep[:, qs_], s, NEG_BIG)
                m_prev = m_sc[i, :, qs_]
                if not rescale:
                    p = jnp.exp2(s - m_prev).astype(BF16)
                    acc_sc[i, :, qs_] = acc_sc[i, :, qs_] + _dot(vtb, p)
                    continue
                m_new = jnp.maximum(m_prev, jnp.max(s, axis=0, keepdims=True))
                a = jnp.exp2(m_prev - m_new)
                p = jnp.exp2(s - m_new).astype(BF16)
                acc_sc[i, :, qs_] = a * acc_sc[i, :, qs_] + _dot(vtb, p)
                m_sc[i, :, qs_] = m_new

    n_wide = (qi * tq) // tk
    n_left = qi - n_wide * (tk // tq)

    def fold(start, width, masked, budget):
        safe = None
        for hh in range(n_h):
            head = hg * n_h + hh
            ok = jnp.logical_and(tile_max(start, width, head) <= budget[hh],
                                 tile_max(start, width, n_heads + head) <= v_ok)
            safe = ok if safe is None else jnp.logical_and(safe, ok)

        def keep_offsets():
            step(start, width, masked, False)
            return budget

        def move_offsets():
            step(start, width, masked, True)
            return norm_budget()

        return lax.cond(safe, keep_offsets, move_offsets)

    def wide(j, budget):
        return fold(pl.multiple_of(j * tk, tk), tk, False, budget)

    def left(j, budget):
        return fold(pl.multiple_of(n_wide * tk + j * tq, tq), tq, False, budget)

    budget = tuple(jnp.where(start_high, k_all[hh], -HUGE) for hh in range(n_h))
    budget = lax.fori_loop(0, n_wide, wide, budget)
    budget = lax.fori_loop(0, n_left, left, budget)
    fold(pl.multiple_of(qi * tq, tq), tq, True, budget)

    lam = lam_ref[...]
    lam_full = (jnp.exp(jnp.sum(lam[0:1] * lam[1:2], axis=-1, keepdims=True))
                - jnp.exp(jnp.sum(lam[2:3] * lam[3:4], axis=-1, keepdims=True))
                + lambda_init)
    for hh in range(n_h):
        num = [acc_sc[2 * hh + c, 0:LANES, :] for c in range(2)]
        inv = [1.0 / acc_sc[2 * hh + c, LANES:LANES + 1, :] for c in range(2)]
        ot = num[0] * inv[0] - lam_full * (num[1] * inv[1])
        ot = ot * lax.rsqrt(jnp.mean(ot * ot, axis=0, keepdims=True) + LN_EPS) * g_ref[...]
        o_ref[0, :, hh * LANES:(hh + 1) * LANES] = (ot * (1.0 - lambda_init)).T.astype(BF16)


def _diff_attention(kmax, qt, k, vt, lam, subln_g, lambda_init):
    b, s, d = k.shape
    tq = min(Q_TILE, s)
    tk = max(tq, min(KV_TILE, s))
    assert tk % tq == 0
    n_h = HEADS_PER_STEP
    gw = n_h * LANES
    assert d % gw == 0
    qt_spec = pl.BlockSpec((1, gw, tq), lambda bi, h, qi, _: (bi, h, qi))
    k_spec = pl.BlockSpec((1, s, gw), lambda bi, h, qi, _: (bi, 0, h))
    vt_spec = pl.BlockSpec((1, n_h * V_ROWS, s), lambda bi, h, qi, _: (bi, h, 0))
    o_spec = pl.BlockSpec((1, tq, gw), lambda bi, h, qi, _: (bi, qi, h))
    g_col = subln_g.reshape(LANES, 1)
    grid_spec = pltpu.PrefetchScalarGridSpec(
        num_scalar_prefetch=1,
        grid=(b, d // gw, s // tq),
        in_specs=[_const_spec(lam.shape), _const_spec(g_col.shape), qt_spec, k_spec, vt_spec],
        out_specs=o_spec,
        scratch_shapes=[pltpu.VMEM((2 * n_h, 1, tq), F32),
                        pltpu.VMEM((2 * n_h, V_ROWS, tq), F32)])
    return pl.pallas_call(
        functools.partial(_attn_kernel, tq=tq, tk=tk, n_h=n_h, n_heads=d // LANES,
                          v_ok=2.0 ** (F32_MAX_EXP - 2 - SAFE_EXP) / s, lambda_init=lambda_init),
        grid_spec=grid_spec,
        out_shape=jax.ShapeDtypeStruct((b, s, d), BF16),
        compiler_params=_params(3),
        name="diff_attn",
    )(kmax, lam, g_col, qt, k, vt)


def _proj_res_ln_kernel(x_ref, a_ref, w_ref, g_ref, b_ref, o_ref, *, alpha):
    h = _dot(a_ref[...], w_ref[...])
    o_ref[...] = _layer_norm(alpha * x_ref[...] + h, g_ref[...], b_ref[...])


def _proj_res_ln(x, a, w, g, b, alpha):
    t, d = x.shape
    tm = min(ROW_TILE, t)
    row = pl.BlockSpec((tm, d), lambda i: (i, 0))
    a_spec = pl.BlockSpec((tm, a.shape[1]), lambda i: (i, 0))
    return pl.pallas_call(
        functools.partial(_proj_res_ln_kernel, alpha=alpha),
        grid=(t // tm,),
        in_specs=[row, a_spec, _const_spec(w.shape), _const_spec(g.shape), _const_spec(b.shape)],
        out_specs=row,
        out_shape=jax.ShapeDtypeStruct((t, d), F32),
        compiler_params=_params(1),
        name="attn_out_ln",
    )(x, a, w, g, b)


def _rope_lane_tables(seq):
    head_dim = LANES // 2
    rot = head_dim // 4
    half = rot // 2
    lane = jnp.arange(LANES, dtype=jnp.int32) % head_dim
    inv_freq = ROPE_THETA ** (-(2 * (lane % half)).astype(F32) / rot)
    inv_freq = jnp.where(lane < rot, inv_freq, 0.0)
    ang = jnp.arange(seq, dtype=F32)[:, None] * inv_freq[None, :]
    c, s = jnp.cos(ang), jnp.sin(ang)
    sa = jnp.where(lane < half, -s, 0.0)
    sb = jnp.where((lane >= half) & (lane < rot), s, 0.0)
    return c, sa, sb


def _ffn_kernel(x_ref, w13_ref, w2_ref, g_ref, b_ref, o_ref, *, alpha, fc):
    d_ff = w2_ref.shape[0]
    rows = x_ref.shape[0] // FFN_STREAMS
    for r0 in range(0, x_ref.shape[0], rows):
        x = x_ref[r0:r0 + rows, :]
        xb = x.astype(BF16)
        acc = None
        for c0 in range(0, d_ff, fc):
            gate = _dot(xb, w13_ref[:, c0:c0 + fc])
            up = _dot(xb, w13_ref[:, d_ff + c0:d_ff + c0 + fc])
            act = (gate * jax.nn.sigmoid(gate) * up).astype(BF16)
            part = _dot(act, w2_ref[c0:c0 + fc, :])
            acc = part if acc is None else acc + part
        o_ref[r0:r0 + rows, :] = _layer_norm(alpha * x + acc, g_ref[...], b_ref[...])


def _ffn(x, w13, w2, g, b, alpha):
    t, d = x.shape
    tm = min(FFN_STREAMS * ROW_TILE, t)
    row = pl.BlockSpec((tm, d), lambda i: (i, 0))
    return pl.pallas_call(
        functools.partial(_ffn_kernel, alpha=alpha, fc=_ffn_chunk(w2.shape[0])),
        grid=(t // tm,),
        in_specs=[row, _const_spec(w13.shape), _const_spec(w2.shape),
                  _const_spec(g.shape), _const_spec(b.shape)],
        out_specs=row,
        out_shape=jax.ShapeDtypeStruct((t, d), F32),
        compiler_params=_params(1),
        name="ffn",
    )(x, w13, w2, g, b)


def _ffn_chunk(d_ff):
    for c in (256, 128):
        if d_ff % c == 0:
            return c
    return d_ff


def _pw1_glu_kernel(x_ref, w_ref, b_ref, o_ref):
    d = x_ref.shape[1]
    h = _dot(x_ref[...].astype(BF16), w_ref[...]) + b_ref[...]
    o_ref[...] = h[:, :d] * jax.nn.sigmoid(h[:, d:])


def _pw1_glu(x, w, b):
    t, d = x.shape
    tm = min(ROW_TILE, t)
    row = pl.BlockSpec((tm, d), lambda i: (i, 0))
    return pl.pallas_call(
        _pw1_glu_kernel,
        grid=(t // tm,),
        in_specs=[row, _const_spec(w.shape), _const_spec(b.shape)],
        out_specs=row,
        out_shape=jax.ShapeDtypeStruct((t, d), F32),
        compiler_params=_params(1),
        name="conv_pw1_glu",
    )(x, w, b)


def _conv_kernel(x_ref, h_ref, halo_ref, dww_ref, dwb_ref, lng_ref, lnb_ref, w_ref, wb_ref,
                 g_ref, b_ref, o_ref, buf, shift_ref, *, alpha, tiles_per_seq):
    tm = x_ref.shape[0]
    width = dww_ref.shape[0]
    first = (pl.program_id(0) % tiles_per_seq) == 0
    halo = halo_ref[...]
    buf[0:CONV_HALO, :] = jnp.where(first, jnp.zeros_like(halo), halo)
    buf[CONV_HALO:, :] = h_ref[...]
    acc = jnp.zeros(h_ref.shape, F32) + dwb_ref[...]
    sub = 8
    n_a = (width + sub - 1) // sub
    top = sub * (n_a - 1)
    assert top + sub <= CONV_HALO
    for b in range(sub):
        if b:
            shift_ref[...] = buf[pl.ds(CONV_HALO - top - b, tm + top), :]
            src, base = shift_ref, top
        else:
            src, base = buf, CONV_HALO
        for a in range(n_a):
            j = sub * a + b
            if j < width:
                acc = acc + (src[pl.ds(base - sub * a, tm), :]
                             * dww_ref[width - 1 - j:width - j, :])
    y = _layer_norm(acc, lng_ref[...], lnb_ref[...])
    y = (y * jax.nn.sigmoid(y)).astype(BF16)
    y = _dot(y, w_ref[...]) + wb_ref[...]
    o_ref[...] = _layer_norm(alpha * x_ref[...] + y, g_ref[...], b_ref[...])


def _conv_tail(x, h, dw_w, dw_b, ln_g, ln_b, pw2_w, pw2_b, g, b, alpha, seq):
    t, d = x.shape
    tm = min(ROW_TILE, seq)
    tiles_per_seq = seq // tm
    ratio = tm // CONV_HALO
    row = pl.BlockSpec((tm, d), lambda i: (i, 0))
    halo = pl.BlockSpec((CONV_HALO, d), lambda i: (jnp.maximum(i * ratio - 1, 0), 0))
    consts = [dw_w, dw_b, ln_g, ln_b, pw2_w, pw2_b, g, b]
    return pl.pallas_call(
        functools.partial(_conv_kernel, alpha=alpha, tiles_per_seq=tiles_per_seq),
        grid=(t // tm,),
        in_specs=[row, row, halo] + [_const_spec(c.shape) for c in consts],
        out_specs=row,
        out_shape=jax.ShapeDtypeStruct((t, d), F32),
        scratch_shapes=[pltpu.VMEM((CONV_HALO + tm, d), F32),
                        pltpu.VMEM((tm + 8 * ((dw_w.shape[0] + 7) // 8 - 1), d), F32)],
        compiler_params=_params(1),
        name="conv_tail",
    )(x, h, h, *consts)


def _pool_kernel(x_ref, halo_ref, w_ref, pb_ref, ps_ref, g_ref, b_ref, o_ref, buf, *,
                 alpha, tiles_per_seq):
    tm, d = x_ref.shape
    n_groups = w_ref.shape[0]
    gw = d // n_groups
    tile = pl.program_id(0) % tiles_per_seq
    x = x_ref[...]
    halo = halo_ref[...]
    buf[0:POOL_HALO, :] = jnp.where(tile == 0, jnp.zeros_like(halo), halo)
    buf[POOL_HALO:, :] = x
    pos = tile * tm + lax.broadcasted_iota(jnp.int32, (tm, 1), 0)
    ys = []
    for gi, win in enumerate(POOL_WINDOWS):
        cols = slice(gi * gw, (gi + 1) * gw)
        tot = x[:, cols]
        for j in range(1, win):
            tot = tot + buf[pl.ds(POOL_HALO - j, tm), cols]
        count = jnp.minimum(pos + 1, win).astype(F32)
        pooled = (tot / count - x[:, cols]).astype(BF16)
        ys.append(_dot(pooled, w_ref[gi]))
    y = (jnp.concatenate(ys, axis=1) + pb_ref[...]) * ps_ref[...]
    o_ref[...] = _layer_norm(alpha * x + y, g_ref[...], b_ref[...])


def _pool_layer(x, pool_w, pool_b, pool_scale, g, b, alpha, seq):
    t, d = x.shape
    tm = min(ROW_TILE, seq)
    tiles_per_seq = seq // tm
    ratio = tm // POOL_HALO
    row = pl.BlockSpec((tm, d), lambda i: (i, 0))
    halo = pl.BlockSpec((POOL_HALO, d), lambda i: (jnp.maximum(i * ratio - 1, 0), 0))
    consts = [pool_w, pool_b, pool_scale, g, b]
    return pl.pallas_call(
        functools.partial(_pool_kernel, alpha=alpha, tiles_per_seq=tiles_per_seq),
        grid=(t // tm,),
        in_specs=[row, halo] + [_const_spec(c.shape) for c in consts],
        out_specs=row,
        out_shape=jax.ShapeDtypeStruct((t, d), F32),
        scratch_shapes=[pltpu.VMEM((POOL_HALO + tm, d), F32)],
        compiler_params=_params(1),
        name="pool_layer",
    )(x, x, *consts)


def kernel(x, attn_wqkv, attn_wo, attn_lambda, attn_subln_g, conv_pw1_w, conv_pw1_b, conv_dw_w, conv_dw_b, conv_ln_g, conv_ln_b, conv_pw2_w, conv_pw2_b, pool_w, pool_b, pool_scale, ffn_w13, ffn_w2, ln1_g, ln1_b, ln2_g, ln2_b):
    bsz, seq, d = x.shape
    depth = ffn_w13.shape[0]
    d_ff = ffn_w2.shape[1]
    alpha = float((2 * depth) ** 0.25)
    head_dim = d // N_HEADS // 2
    assert 2 * head_dim == LANES and seq % min(ROW_TILE, seq) == 0
    row2 = lambda v: v.reshape(1, -1)

    cos_t, sa_t, sb_t = _rope_lane_tables(seq)

    h = x.reshape(bsz * seq, d)
    for i in range(depth):
        mixer, j = i % N_MIXERS, i // N_MIXERS
        g1, b1 = row2(ln1_g[i]), row2(ln1_b[i])
        if mixer == 0:
            lambda_init = 0.8 - 0.6 * math.exp(-0.3 * i)
            qt, k, vt, kmax = _qkv_rope(h, attn_wqkv[j].astype(BF16), cos_t, sa_t, sb_t, bsz, seq,
                                        head_dim ** -0.5 * math.log2(math.e))
            tile_stats = kmax[:, :2, :d // LANES]
            k_all = jnp.max(tile_stats[:, 0, :].reshape(bsz, -1, d // LANES), axis=1)
            stats = jnp.concatenate([tile_stats.reshape(-1), k_all.reshape(-1)])
            o = _diff_attention(stats, qt, k.reshape(bsz, seq, d), vt,
                                attn_lambda[j], attn_subln_g[j], lambda_init)
            h = _proj_res_ln(h, o.reshape(bsz * seq, d), attn_wo[j].astype(BF16), g1, b1, alpha)
        elif mixer == 1:
            glu = _pw1_glu(h, conv_pw1_w[j].astype(BF16), row2(conv_pw1_b[j]))
            h = _conv_tail(h, glu, conv_dw_w[j], row2(conv_dw_b[j]), row2(conv_ln_g[j]),
                           row2(conv_ln_b[j]), conv_pw2_w[j].astype(BF16), row2(conv_pw2_b[j]),
                           g1, b1, alpha, seq)
        else:
            h = _pool_layer(h, pool_w[j].astype(BF16), row2(pool_b[j]), row2(pool_scale[j]),
                            g1, b1, alpha, seq)
        h = _ffn(h, ffn_w13[i].astype(BF16), ffn_w2[i].astype(BF16),
                 row2(ln2_g[i]), row2(ln2_b[i]), alpha)
    return h.reshape(bsz, seq, d)
```

```python
import functools
import math

import jax
import jax.numpy as jnp
from jax import lax
from jax.experimental import pallas as pl
from jax.experimental.pallas import tpu as pltpu

F32 = jnp.float32
BF16 = jnp.bfloat16

N_MIXERS = 3
N_HEADS = 8
ROPE_THETA = 500000.0
POOL_WINDOWS = (2, 4, 8, 16)
LN_EPS = 1e-5
NEG_BIG = -1e30
HUGE = 3e38
SAFE_EXP = 60.0
F32_MAX_EXP = 127
NORM_SLACK = 1.01
INIT_BOUND_MAX = 90.0

LANES = 128
VMEM_LIMIT = 56 * 1024 * 1024
ROW_TILE = 512
FFN_STREAMS = 2
Q_TILE = 512
KV_TILE = 2048
HEADS_PER_STEP = 2
SUM_ROWS = 16
V_ROWS = LANES + SUM_ROWS
CONV_HALO = 32
POOL_HALO = 16


def _params(n_axes, flags=None):
    return pltpu.CompilerParams(dimension_semantics=("arbitrary",) * n_axes,
                                vmem_limit_bytes=VMEM_LIMIT, flags=flags)


def _const_spec(shape):
    nd = len(shape)
    return pl.BlockSpec(shape, lambda *_: (0,) * nd, pipeline_mode=pl.Buffered(1))


def _dot(a, b):
    return jnp.dot(a, b, preferred_element_type=F32)


def _layer_norm(y, g, b):
    mu = jnp.mean(y, axis=-1, keepdims=True)
    yc = y - mu
    var = jnp.mean(yc * yc, axis=-1, keepdims=True)
    return yc * lax.rsqrt(var + LN_EPS) * g + b


def _qkv_rope_kernel(x_ref, w_ref, cos_ref, sa_ref, sb_ref, qt_ref, k_ref, vt_ref, kmax_ref, *,
                     q_scale):
    d = x_ref.shape[1]
    half_rot = 8
    xb = x_ref[...].astype(BF16)
    c = cos_ref[...]
    sa = sa_ref[...]
    sb = sb_ref[...]

    def rope(y):
        return (y * c + pltpu.roll(y, LANES - half_rot, 1) * sa
                + pltpu.roll(y, half_rot, 1) * sb)

    yq = _dot(xb, w_ref[:, 0:d])
    for h in range(d // LANES):
        sl = slice(h * LANES, (h + 1) * LANES)
        qt_ref[0, sl, :] = (rope(yq[:, sl]) * q_scale).T.astype(BF16)
    yk = _dot(xb, w_ref[:, d:2 * d])
    ksq = []
    for h in range(d // LANES):
        sl = slice(h * LANES, (h + 1) * LANES)
        kh = rope(yk[:, sl]).astype(BF16)
        k_ref[:, sl] = kh
        k32 = kh.astype(F32)
        ksq.append((k32 * k32).astype(BF16))
    head_of_feat = lax.broadcasted_iota(jnp.int32, (d, LANES), 0) // LANES
    col_id = lax.broadcasted_iota(jnp.int32, (d, LANES), 1)
    indicator = jnp.where(head_of_feat == col_id, 1.0, 0.0).astype(BF16)
    nsq = _dot(jnp.concatenate(ksq, axis=1), indicator)
    k_row = jnp.sqrt(jnp.max(nsq, axis=0, keepdims=True)) * NORM_SLACK
    yv = _dot(xb, w_ref[:, 2 * d:3 * d])
    v_abs = jnp.max(jnp.abs(yv), axis=0, keepdims=True)
    lane_id = lax.broadcasted_iota(jnp.int32, (1, LANES), 1)
    v_row = jnp.zeros((1, LANES), F32)
    ones = jnp.ones((SUM_ROWS, x_ref.shape[0]), BF16)
    for h in range(d // LANES):
        r0 = h * V_ROWS
        vh = yv[:, h * LANES:(h + 1) * LANES]
        vt_ref[0, r0:r0 + LANES, :] = vh.T.astype(BF16)
        vt_ref[0, r0 + LANES:r0 + V_ROWS, :] = ones
        v_max = jnp.max(v_abs[:, h * LANES:(h + 1) * LANES], axis=-1, keepdims=True)
        v_row = jnp.where(lane_id == h, v_max, v_row)
    kmax_ref[0] = jnp.concatenate([k_row, v_row, jnp.zeros((6, LANES), F32)], axis=0)


def _qkv_rope(x, wqkv, cos_t, sa_t, sb_t, bsz, seq, q_scale):
    t, d = x.shape
    tm = min(Q_TILE, seq)
    tiles_per_seq = seq // tm
    n_heads = d // LANES
    dv = n_heads * V_ROWS
    row = pl.BlockSpec((tm, d), lambda i: (i, 0))
    col = pl.BlockSpec((1, d, tm), lambda i: (i // tiles_per_seq, 0, i % tiles_per_seq))
    col_v = pl.BlockSpec((1, dv, tm), lambda i: (i // tiles_per_seq, 0, i % tiles_per_seq))
    tab = pl.BlockSpec((tm, LANES), lambda i: (i % tiles_per_seq, 0))
    kmx = pl.BlockSpec((1, 8, LANES), lambda i: (i, 0, 0))
    return pl.pallas_call(
        functools.partial(_qkv_rope_kernel, q_scale=q_scale),
        grid=(t // tm,),
        in_specs=[row, _const_spec(wqkv.shape), tab, tab, tab],
        out_specs=[col, row, col_v, kmx],
        out_shape=[jax.ShapeDtypeStruct((bsz, d, seq), BF16), jax.ShapeDtypeStruct((t, d), BF16),
                   jax.ShapeDtypeStruct((bsz, dv, seq), BF16),
                   jax.ShapeDtypeStruct((t // tm, 8, LANES), F32)],
        compiler_params=_params(1),
        name="qkv_rope",
    )(x, wqkv, cos_t, sa_t, sb_t)


def _attn_kernel(kmax_ref, lam_ref, g_ref, qt_ref, k_ref, vt_ref, o_ref, m_sc, acc_sc, *,
                 tq, tk, n_h, n_heads, v_ok, lambda_init):
    bi, hg, qi = pl.program_id(0), pl.program_id(1), pl.program_id(2)
    hd = LANES // 2
    cw = min(2 * LANES, tq)
    feat = lax.broadcasted_iota(jnp.int32, (LANES, tq), 0)
    qts, qnorm = [], []
    for hh in range(n_h):
        qt = qt_ref[0, hh * LANES:(hh + 1) * LANES, :]
        zero = jnp.zeros_like(qt)
        qts.append(jnp.concatenate([jnp.where(feat < hd, qt, zero), jnp.where(feat >= hd, qt, zero)],
                                   axis=1))
        q32 = qt.astype(F32)
        qnorm.append(jnp.sqrt(jnp.sum(q32 * q32, axis=0, keepdims=True)))
    acc_sc[...] = jnp.zeros(acc_sc.shape, F32)

    n_tiles = pl.num_programs(0) * (k_ref.shape[1] // tq)
    k_all = [kmax_ref[n_tiles * 2 * n_heads + bi * n_heads + hg * n_h + hh] for hh in range(n_h)]
    start_high = None
    for hh in range(n_h):
        ok = k_all[hh] * jnp.max(qnorm[hh]) <= INIT_BOUND_MAX
        start_high = ok if start_high is None else jnp.logical_and(start_high, ok)
    for hh in range(n_h):
        m0 = jnp.where(start_high, k_all[hh] * qnorm[hh] - SAFE_EXP, NEG_BIG)
        m_sc[2 * hh] = m0
        m_sc[2 * hh + 1] = m0

    def tile_max(start, width, row):
        base = (bi * (k_ref.shape[1] // tq) + start // tq) * 2 * n_heads + row
        out = kmax_ref[base]
        for t in range(1, width // tq):
            out = jnp.maximum(out, kmax_ref[base + t * 2 * n_heads])
        return out

    def norm_budget():
        out = []
        for hh in range(n_h):
            room = jnp.minimum(m_sc[2 * hh], m_sc[2 * hh + 1]) + SAFE_EXP
            ok = qnorm[hh] > 0.0
            ratio = jnp.where(ok, room / jnp.where(ok, qnorm[hh], 1.0),
                              jnp.where(room >= 0.0, HUGE, -HUGE))
            out.append(jnp.min(ratio))
        return tuple(out)

    def step(start, width, masked, rescale):
        if masked:
            k_pos = start + lax.broadcasted_iota(jnp.int32, (width, tq), 0)
            q_pos = qi * tq + lax.broadcasted_iota(jnp.int32, (width, tq), 1)
            keep = k_pos <= q_pos
        scores = []
        for hh in range(n_h):
            kb = k_ref[0, pl.ds(start, width), hh * LANES:(hh + 1) * LANES]
            scores.append(_dot(kb, qts[hh]))
        for hh in range(n_h):
            vtb = vt_ref[0, hh * V_ROWS:(hh + 1) * V_ROWS, pl.ds(start, width)]
            for c, q0 in [(c, q0) for c in range(2) for q0 in range(0, tq, cw)]:
                i = 2 * hh + c
                qs_ = slice(q0, q0 + cw)
                s = scores[hh][:, c * tq + q0:c * tq + q0 + cw]
                if masked:
                    s = jnp.where(keep[:, qs_], s, NEG_BIG)
                m_prev = m_sc[i, :, qs_]
                if not rescale:
                    p = jnp.exp2(s - m_prev).astype(BF16)
                    acc_sc[i, :, qs_] = acc_sc[i, :, qs_] + _dot(vtb, p)
                    continue
                m_new = jnp.maximum(m_prev, jnp.max(s, axis=0, keepdims=True))
                a = jnp.exp2(m_prev - m_new)
                p = jnp.exp2(s - m_new).astype(BF16)
                acc_sc[i, :, qs_] = a * acc_sc[i, :, qs_] + _dot(vtb, p)
                m_sc[i, :, qs_] = m_new

    n_wide = (qi * tq) // tk
    n_left = qi - n_wide * (tk // tq)

    def fold(start, width, masked, budget):
        safe = None
        for hh in range(n_h):
            head = hg * n_h + hh
            ok = jnp.logical_and(tile_max(start, width, head) <= budget[hh],
                                 tile_max(start, width, n_heads + head) <= v_ok)
            safe = ok if safe is None else jnp.logical_and(safe, ok)

        def keep_offsets():
            step(start, width, masked, False)
            return budget

        def move_offsets():
            step(start, width, masked, True)
            return norm_budget()

        return lax.cond(safe, keep_offsets, move_offsets)

    def wide(j, budget):
        return fold(pl.multiple_of(j * tk, tk), tk, False, budget)

    def left(j, budget):
        return fold(pl.multiple_of(n_wide * tk + j * tq, tq), tq, False, budget)

    budget = tuple(jnp.where(start_high, k_all[hh], -HUGE) for hh in range(n_h))
    budget = lax.fori_loop(0, n_wide, wide, budget)
    budget = lax.fori_loop(0, n_left, left, budget)
    fold(pl.multiple_of(qi * tq, tq), tq, True, budget)

    lam = lam_ref[...]
    lam_full = (jnp.exp(jnp.sum(lam[0:1] * lam[1:2], axis=-1, keepdims=True))
                - jnp.exp(jnp.sum(lam[2:3] * lam[3:4], axis=-1, keepdims=True))
                + lambda_init)
    for hh in range(n_h):
        num = [acc_sc[2 * hh + c, 0:LANES, :] for c in range(2)]
        inv = [1.0 / acc_sc[2 * hh + c, LANES:LANES + 1, :] for c in range(2)]
        ot = num[0] * inv[0] - lam_full * (num[1] * inv[1])
        ot = ot * lax.rsqrt(jnp.mean(ot * ot, axis=0, keepdims=True) + LN_EPS) * g_ref[...]
        o_ref[0, :, hh * LANES:(hh + 1) * LANES] = (ot * (1.0 - lambda_init)).T.astype(BF16)


def _diff_attention(kmax, qt, k, vt, lam, subln_g, lambda_init):
    b, s, d = k.shape
    tq = min(Q_TILE, s)
    tk = max(tq, min(KV_TILE, s))
    assert tk % tq == 0
    n_h = HEADS_PER_STEP
    gw = n_h * LANES
    assert d % gw == 0
    qt_spec = pl.BlockSpec((1, gw, tq), lambda bi, h, qi, _: (bi, h, qi))
    k_spec = pl.BlockSpec((1, s, gw), lambda bi, h, qi, _: (bi, 0, h))
    vt_spec = pl.BlockSpec((1, n_h * V_ROWS, s), lambda bi, h, qi, _: (bi, h, 0))
    o_spec = pl.BlockSpec((1, tq, gw), lambda bi, h, qi, _: (bi, qi, h))
    g_col = subln_g.reshape(LANES, 1)
    grid_spec = pltpu.PrefetchScalarGridSpec(
        num_scalar_prefetch=1,
        grid=(b, d // gw, s // tq),
        in_specs=[_const_spec(lam.shape), _const_spec(g_col.shape), qt_spec, k_spec, vt_spec],
        out_specs=o_spec,
        scratch_shapes=[pltpu.VMEM((2 * n_h, 1, tq), F32),
                        pltpu.VMEM((2 * n_h, V_ROWS, tq), F32)])
    return pl.pallas_call(
        functools.partial(_attn_kernel, tq=tq, tk=tk, n_h=n_h, n_heads=d // LANES,
                          v_ok=2.0 ** (F32_MAX_EXP - 2 - SAFE_EXP) / s, lambda_init=lambda_init),
        grid_spec=grid_spec,
        out_shape=jax.ShapeDtypeStruct((b, s, d), BF16),
        compiler_params=_params(3),
        name="diff_attn",
    )(kmax, lam, g_col, qt, k, vt)


def _proj_res_ln_kernel(x_ref, a_ref, w_ref, g_ref, b_ref, o_ref, *, alpha):
    h = _dot(a_ref[...], w_ref[...])
    o_ref[...] = _layer_norm(alpha * x_ref[...] + h, g_ref[...], b_ref[...])


def _proj_res_ln(x, a, w, g, b, alpha):
    t, d = x.shape
    tm = min(ROW_TILE, t)
    row = pl.BlockSpec((tm, d), lambda i: (i, 0))
    a_spec = pl.BlockSpec((tm, a.shape[1]), lambda i: (i, 0))
    return pl.pallas_call(
        functools.partial(_proj_res_ln_kernel, alpha=alpha),
        grid=(t // tm,),
        in_specs=[row, a_spec, _const_spec(w.shape), _const_spec(g.shape), _const_spec(b.shape)],
        out_specs=row,
        out_shape=jax.ShapeDtypeStruct((t, d), F32),
        compiler_params=_params(1),
        name="attn_out_ln",
    )(x, a, w, g, b)


def _rope_lane_tables(seq):
    head_dim = LANES // 2
    rot = head_dim // 4
    half = rot // 2
    lane = jnp.arange(LANES, dtype=jnp.int32) % head_dim
    inv_freq = ROPE_THETA ** (-(2 * (lane % half)).astype(F32) / rot)
    inv_freq = jnp.where(lane < rot, inv_freq, 0.0)
    ang = jnp.arange(seq, dtype=F32)[:, None] * inv_freq[None, :]
    c, s = jnp.cos(ang), jnp.sin(ang)
    sa = jnp.where(lane < half, -s, 0.0)
    sb = jnp.where((lane >= half) & (lane < rot), s, 0.0)
    return c, sa, sb


def _ffn_kernel(x_ref, w13_ref, w2_ref, g_ref, b_ref, o_ref, *, alpha, fc):
    d_ff = w2_ref.shape[0]
    rows = x_ref.shape[0] // FFN_STREAMS
    for r0 in range(0, x_ref.shape[0], rows):
        x = x_ref[r0:r0 + rows, :]
        xb = x.astype(BF16)
        acc = None
        for c0 in range(0, d_ff, fc):
            gate = _dot(xb, w13_ref[:, c0:c0 + fc])
            up = _dot(xb, w13_ref[:, d_ff + c0:d_ff + c0 + fc])
            act = (gate * jax.nn.sigmoid(gate) * up).astype(BF16)
            part = _dot(act, w2_ref[c0:c0 + fc, :])
            acc = part if acc is None else acc + part
        o_ref[r0:r0 + rows, :] = _layer_norm(alpha * x + acc, g_ref[...], b_ref[...])


def _ffn(x, w13, w2, g, b, alpha):
    t, d = x.shape
    tm = min(FFN_STREAMS * ROW_TILE, t)
    row = pl.BlockSpec((tm, d), lambda i: (i, 0))
    return pl.pallas_call(
        functools.partial(_ffn_kernel, alpha=alpha, fc=_ffn_chunk(w2.shape[0])),
        grid=(t // tm,),
        in_specs=[row, _const_spec(w13.shape), _const_spec(w2.shape),
                  _const_spec(g.shape), _const_spec(b.shape)],
        out_specs=row,
        out_shape=jax.ShapeDtypeStruct((t, d), F32),
        compiler_params=_params(1),
        name="ffn",
    )(x, w13, w2, g, b)


def _ffn_chunk(d_ff):
    for c in (256, 128):
        if d_ff % c == 0:
            return c
    return d_ff


def _pw1_glu_kernel(x_ref, w_ref, b_ref, o_ref):
    d = x_ref.shape[1]
    h = _dot(x_ref[...].astype(BF16), w_ref[...]) + b_ref[...]
    o_ref[...] = h[:, :d] * jax.nn.sigmoid(h[:, d:])


def _pw1_glu(x, w, b):
    t, d = x.shape
    tm = min(ROW_TILE, t)
    row = pl.BlockSpec((tm, d), lambda i: (i, 0))
    return pl.pallas_call(
        _pw1_glu_kernel,
        grid=(t // tm,),
        in_specs=[row, _const_spec(w.shape), _const_spec(b.shape)],
        out_specs=row,
        out_shape=jax.ShapeDtypeStruct((t, d), F32),
        compiler_params=_params(1),
        name="conv_pw1_glu",
    )(x, w, b)


def _conv_kernel(x_ref, h_ref, halo_ref, dww_ref, dwb_ref, lng_ref, lnb_ref, w_ref, wb_ref,
                 g_ref, b_ref, o_ref, buf, shift_ref, *, alpha, tiles_per_seq):
    tm = x_ref.shape[0]
    width = dww_ref.shape[0]
    first = (pl.program_id(0) % tiles_per_seq) == 0
    halo = halo_ref[...]
    buf[0:CONV_HALO, :] = jnp.where(first, jnp.zeros_like(halo), halo)
    buf[CONV_HALO:, :] = h_ref[...]
    acc = jnp.zeros(h_ref.shape, F32) + dwb_ref[...]
    sub = 8
    n_a = (width + sub - 1) // sub
    top = sub * (n_a - 1)
    assert top + sub <= CONV_HALO
    for b in range(sub):
        if b:
            shift_ref[...] = buf[pl.ds(CONV_HALO - top - b, tm + top), :]
            src, base = shift_ref, top
        else:
            src, base = buf, CONV_HALO
        for a in range(n_a):
            j = sub * a + b
            if j < width:
                acc = acc + (src[pl.ds(base - sub * a, tm), :]
                             * dww_ref[width - 1 - j:width - j, :])
    y = _layer_norm(acc, lng_ref[...], lnb_ref[...])
    y = (y * jax.nn.sigmoid(y)).astype(BF16)
    y = _dot(y, w_ref[...]) + wb_ref[...]
    o_ref[...] = _layer_norm(alpha * x_ref[...] + y, g_ref[...], b_ref[...])


def _conv_tail(x, h, dw_w, dw_b, ln_g, ln_b, pw2_w, pw2_b, g, b, alpha, seq):
    t, d = x.shape
    tm = min(ROW_TILE, seq)
    tiles_per_seq = seq // tm
    ratio = tm // CONV_HALO
    row = pl.BlockSpec((tm, d), lambda i: (i, 0))
    halo = pl.BlockSpec((CONV_HALO, d), lambda i: (jnp.maximum(i * ratio - 1, 0), 0))
    consts = [dw_w, dw_b, ln_g, ln_b, pw2_w, pw2_b, g, b]
    return pl.pallas_call(
        functools.partial(_conv_kernel, alpha=alpha, tiles_per_seq=tiles_per_seq),
        grid=(t // tm,),
        in_specs=[row, row, halo] + [_const_spec(c.shape) for c in consts],
        out_specs=row,
        out_shape=jax.ShapeDtypeStruct((t, d), F32),
        scratch_shapes=[pltpu.VMEM((CONV_HALO + tm, d), F32),
                        pltpu.VMEM((tm + 8 * ((dw_w.shape[0] + 7) // 8 - 1), d), F32)],
        compiler_params=_params(1),
        name="conv_tail",
    )(x, h, h, *consts)


def _pool_kernel(x_ref, halo_ref, w_ref, pb_ref, ps_ref, g_ref, b_ref, o_ref, buf, *,
                 alpha, tiles_per_seq):
    tm, d = x_ref.shape
    n_groups = w_ref.shape[0]
    gw = d // n_groups
    tile = pl.program_id(0) % tiles_per_seq
    x = x_ref[...]
    halo = halo_ref[...]
    buf[0:POOL_HALO, :] = jnp.where(tile == 0, jnp.zeros_like(halo), halo)
    buf[POOL_HALO:, :] = x
    pos = tile * tm + lax.broadcasted_iota(jnp.int32, (tm, 1), 0)
    ys = []
    for gi, win in enumerate(POOL_WINDOWS):
        cols = slice(gi * gw, (gi + 1) * gw)
        tot = x[:, cols]
        for j in range(1, win):
            tot = tot + buf[pl.ds(POOL_HALO - j, tm), cols]
        count = jnp.minimum(pos + 1, win).astype(F32)
        pooled = (tot / count - x[:, cols]).astype(BF16)
        ys.append(_dot(pooled, w_ref[gi]))
    y = (jnp.concatenate(ys, axis=1) + pb_ref[...]) * ps_ref[...]
    o_ref[...] = _layer_norm(alpha * x + y, g_ref[...], b_ref[...])


def _pool_layer(x, pool_w, pool_b, pool_scale, g, b, alpha, seq):
    t, d = x.shape
    tm = min(ROW_TILE, seq)
    tiles_per_seq = seq // tm
    ratio = tm // POOL_HALO
    row = pl.BlockSpec((tm, d), lambda i: (i, 0))
    halo = pl.BlockSpec((POOL_HALO, d), lambda i: (jnp.maximum(i * ratio - 1, 0), 0))
    consts = [pool_w, pool_b, pool_scale, g, b]
    return pl.pallas_call(
        functools.partial(_pool_kernel, alpha=alpha, tiles_per_seq=tiles_per_seq),
        grid=(t // tm,),
        in_specs=[row, halo] + [_const_spec(c.shape) for c in consts],
        out_specs=row,
        out_shape=jax.ShapeDtypeStruct((t, d), F32),
        scratch_shapes=[pltpu.VMEM((POOL_HALO + tm, d), F32)],
        compiler_params=_params(1),
        name="pool_layer",
    )(x, x, *consts)


def kernel(x, attn_wqkv, attn_wo, attn_lambda, attn_subln_g, conv_pw1_w, conv_pw1_b, conv_dw_w, conv_dw_b, conv_ln_g, conv_ln_b, conv_pw2_w, conv_pw2_b, pool_w, pool_b, pool_scale, ffn_w13, ffn_w2, ln1_g, ln1_b, ln2_g, ln2_b):
    bsz, seq, d = x.shape
    depth = ffn_w13.shape[0]
    d_ff = ffn_w2.shape[1]
    alpha = float((2 * depth) ** 0.25)
    head_dim = d // N_HEADS // 2
    assert 2 * head_dim == LANES and seq % min(ROW_TILE, seq) == 0
    row2 = lambda v: v.reshape(1, -1)

    cos_t, sa_t, sb_t = _rope_lane_tables(seq)

    h = x.reshape(bsz * seq, d)
    for i in range(depth):
        mixer, j = i % N_MIXERS, i // N_MIXERS
        g1, b1 = row2(ln1_g[i]), row2(ln1_b[i])
        if mixer == 0:
            lambda_init = 0.8 - 0.6 * math.exp(-0.3 * i)
            qt, k, vt, kmax = _qkv_rope(h, attn_wqkv[j].astype(BF16), cos_t, sa_t, sb_t, bsz, seq,
                                        head_dim ** -0.5 * math.log2(math.e))
            tile_stats = kmax[:, :2, :d // LANES]
            k_all = jnp.max(tile_stats[:, 0, :].reshape(bsz, -1, d // LANES), axis=1)
            stats = jnp.concatenate([tile_stats.reshape(-1), k_all.reshape(-1)])
            o = _diff_attention(stats, qt, k.reshape(bsz, seq, d), vt,
                                attn_lambda[j], attn_subln_g[j], lambda_init)
            h = _proj_res_ln(h, o.reshape(bsz * seq, d), attn_wo[j].astype(BF16), g1, b1, alpha)
        elif mixer == 1:
            glu = _pw1_glu(h, conv_pw1_w[j].astype(BF16), row2(conv_pw1_b[j]))
            h = _conv_tail(h, glu, conv_dw_w[j], row2(conv_dw_b[j]), row2(conv_ln_g[j]),
                           row2(conv_ln_b[j]), conv_pw2_w[j].astype(BF16), row2(conv_pw2_b[j]),
                           g1, b1, alpha, seq)
        else:
            h = _pool_layer(h, pool_w[j].astype(BF16), row2(pool_b[j]), row2(pool_scale[j]),
                            g1, b1, alpha, seq)
        h = _ffn(h, ffn_w13[i].astype(BF16), ffn_w2[i].astype(BF16),
                 row2(ln2_g[i]), row2(ln2_b[i]), alpha)
    return h.reshape(bsz, seq, d)
```

```python
import functools
import math

import jax
import jax.numpy as jnp
from jax import lax
from jax.experimental import pallas as pl
from jax.experimental.pallas import tpu as pltpu

F32 = jnp.float32
BF16 = jnp.bfloat16

N_MIXERS = 3
N_HEADS = 8
ROPE_THETA = 500000.0
POOL_WINDOWS = (2, 4, 8, 16)
LN_EPS = 1e-5
NEG_BIG = -1e30
HUGE = 3e38
SAFE_EXP = 60.0
F32_MAX_EXP = 127
NORM_SLACK = 1.01
INIT_BOUND_MAX = 90.0

LANES = 128
VMEM_LIMIT = 56 * 1024 * 1024
ROW_TILE = 512
FFN_STREAMS = 2
Q_TILE = 512
KV_TILE = 2048
HEADS_PER_STEP = 2
SUM_ROWS = 16
V_ROWS = LANES + SUM_ROWS
CONV_HALO = 32
POOL_HALO = 16


def _params(n_axes, flags=None):
    return pltpu.CompilerParams(dimension_semantics=("arbitrary",) * n_axes,
                                vmem_limit_bytes=VMEM_LIMIT, flags=flags)


def _const_spec(shape):
    nd = len(shape)
    return pl.BlockSpec(shape, lambda *_: (0,) * nd, pipeline_mode=pl.Buffered(1))


def _dot(a, b):
    return jnp.dot(a, b, preferred_element_type=F32)


def _layer_norm(y, g, b):
    mu = jnp.mean(y, axis=-1, keepdims=True)
    yc = y - mu
    var = jnp.mean(yc * yc, axis=-1, keepdims=True)
    return yc * lax.rsqrt(var + LN_EPS) * g + b


def _qkv_rope_kernel(x_ref, w_ref, cos_ref, sa_ref, sb_ref, qt_ref, k_ref, vt_ref, kmax_ref, *,
                     q_scale):
    d = x_ref.shape[1]
    half_rot = 8
    xb = x_ref[...].astype(BF16)
    c = cos_ref[...]
    sa = sa_ref[...]
    sb = sb_ref[...]

    def rope(y):
        return (y * c + pltpu.roll(y, LANES - half_rot, 1) * sa
                + pltpu.roll(y, half_rot, 1) * sb)

    yq = _dot(xb, w_ref[:, 0:d])
    for h in range(d // LANES):
        sl = slice(h * LANES, (h + 1) * LANES)
        qt_ref[0, sl, :] = (rope(yq[:, sl]) * q_scale).T.astype(BF16)
    yk = _dot(xb, w_ref[:, d:2 * d])
    ksq = []
    for h in range(d // LANES):
        sl = slice(h * LANES, (h + 1) * LANES)
        kh = rope(yk[:, sl]).astype(BF16)
        k_ref[:, sl] = kh
        k32 = kh.astype(F32)
        ksq.append((k32 * k32).astype(BF16))
    head_of_feat = lax.broadcasted_iota(jnp.int32, (d, LANES), 0) // LANES
    col_id = lax.broadcasted_iota(jnp.int32, (d, LANES), 1)
    indicator = jnp.where(head_of_feat == col_id, 1.0, 0.0).astype(BF16)
    nsq = _dot(jnp.concatenate(ksq, axis=1), indicator)
    k_row = jnp.sqrt(jnp.max(nsq, axis=0, keepdims=True)) * NORM_SLACK
    yv = _dot(xb, w_ref[:, 2 * d:3 * d])
    v_abs = jnp.max(jnp.abs(yv), axis=0, keepdims=True)
    lane_id = lax.broadcasted_iota(jnp.int32, (1, LANES), 1)
    v_row = jnp.zeros((1, LANES), F32)
    ones = jnp.ones((SUM_ROWS, x_ref.shape[0]), BF16)
    for h in range(d // LANES):
        r0 = h * V_ROWS
        vh = yv[:, h * LANES:(h + 1) * LANES]
        vt_ref[0, r0:r0 + LANES, :] = vh.T.astype(BF16)
        vt_ref[0, r0 + LANES:r0 + V_ROWS, :] = ones
        v_max = jnp.max(v_abs[:, h * LANES:(h + 1) * LANES], axis=-1, keepdims=True)
        v_row = jnp.where(lane_id == h, v_max, v_row)
    kmax_ref[0] = jnp.concatenate([k_row, v_row, jnp.zeros((6, LANES), F32)], axis=0)


def _qkv_rope(x, wqkv, cos_t, sa_t, sb_t, bsz, seq, q_scale):
    t, d = x.shape
    tm = min(Q_TILE, seq)
    tiles_per_seq = seq // tm
    n_heads = d // LANES
    dv = n_heads * V_ROWS
    row = pl.BlockSpec((tm, d), lambda i: (i, 0))
    col = pl.BlockSpec((1, d, tm), lambda i: (i // tiles_per_seq, 0, i % tiles_per_seq))
    col_v = pl.BlockSpec((1, dv, tm), lambda i: (i // tiles_per_seq, 0, i % tiles_per_seq))
    tab = pl.BlockSpec((tm, LANES), lambda i: (i % tiles_per_seq, 0))
    kmx = pl.BlockSpec((1, 8, LANES), lambda i: (i, 0, 0))
    return pl.pallas_call(
        functools.partial(_qkv_rope_kernel, q_scale=q_scale),
        grid=(t // tm,),
        in_specs=[row, _const_spec(wqkv.shape), tab, tab, tab],
        out_specs=[col, row, col_v, kmx],
        out_shape=[jax.ShapeDtypeStruct((bsz, d, seq), BF16), jax.ShapeDtypeStruct((t, d), BF16),
                   jax.ShapeDtypeStruct((bsz, dv, seq), BF16),
                   jax.ShapeDtypeStruct((t // tm, 8, LANES), F32)],
        compiler_params=_params(1),
        name="qkv_rope",
    )(x, wqkv, cos_t, sa_t, sb_t)


def _attn_kernel(kmax_ref, lam_ref, g_ref, qt_ref, k_ref, vt_ref, o_ref, m_sc, acc_sc, *,
                 tq, tk, n_h, n_heads, v_ok, lambda_init):
    bi, hg, qi = pl.program_id(0), pl.program_id(1), pl.program_id(2)
    hd = LANES // 2
    cw = min(2 * LANES, tq)
    feat = lax.broadcasted_iota(jnp.int32, (LANES, tq), 0)
    qts, qnorm = [], []
    for hh in range(n_h):
        qt = qt_ref[0, hh * LANES:(hh + 1) * LANES, :]
        zero = jnp.zeros_like(qt)
        qts.append(jnp.concatenate([jnp.where(feat < hd, qt, zero), jnp.where(feat >= hd, qt, zero)],
                                   axis=1))
        q32 = qt.astype(F32)
        qnorm.append(jnp.sqrt(jnp.sum(q32 * q32, axis=0, keepdims=True)))
    acc_sc[...] = jnp.zeros(acc_sc.shape, F32)

    n_tiles = pl.num_programs(0) * (k_ref.shape[1] // tq)
    k_all = [kmax_ref[n_tiles * 2 * n_heads + bi * n_heads + hg * n_h + hh] for hh in range(n_h)]
    start_high = None
    for hh in range(n_h):
        ok = k_all[hh] * jnp.max(qnorm[hh]) <= INIT_BOUND_MAX
        start_high = ok if start_high is None else jnp.logical_and(start_high, ok)
    for hh in range(n_h):
        m0 = jnp.where(start_high, k_all[hh] * qnorm[hh] - SAFE_EXP, NEG_BIG)
        m_sc[2 * hh] = m0
        m_sc[2 * hh + 1] = m0

    def tile_max(start, width, row):
        base = (bi * (k_ref.shape[1] // tq) + start // tq) * 2 * n_heads + row
        out = kmax_ref[base]
        for t in range(1, width // tq):
            out = jnp.maximum(out, kmax_ref[base + t * 2 * n_heads])
        return out

    def norm_budget():
        out = []
        for hh in range(n_h):
            room = jnp.minimum(m_sc[2 * hh], m_sc[2 * hh + 1]) + SAFE_EXP
            ok = qnorm[hh] > 0.0
            ratio = jnp.where(ok, room / jnp.where(ok, qnorm[hh], 1.0),
                              jnp.where(room >= 0.0, HUGE, -HUGE))
            out.append(jnp.min(ratio))
        return tuple(out)

    def step(start, width, masked, rescale):
        if masked:
            k_pos = start + lax.broadcasted_iota(jnp.int32, (width, tq), 0)
            q_pos = qi * tq + lax.broadcasted_iota(jnp.int32, (width, tq), 1)
            keep = k_pos <= q_pos
        scores = []
        for hh in range(n_h):
            kb = k_ref[0, pl.ds(start, width), hh * LANES:(hh + 1) * LANES]
            scores.append(_dot(kb, qts[hh]))
        for hh in range(n_h):
            vtb = vt_ref[0, hh * V_ROWS:(hh + 1) * V_ROWS, pl.ds(start, width)]
            for c, q0 in [(c, q0) for c in range(2) for q0 in range(0, tq, cw)]:
                i = 2 * hh + c
                qs_ = slice(q0, q0 + cw)
                s = scores[hh][:, c * tq + q0:c * tq + q0 + cw]
                if masked:
                    s = jnp.where(keep[:, qs_], s, NEG_BIG)
                m_prev = m_sc[i, :, qs_]
                if not rescale:
                    p = jnp.exp2(s - m_prev).astype(BF16)
                    acc_sc[i, :, qs_] = acc_sc[i, :, qs_] + _dot(vtb, p)
                    continue
                m_new = jnp.maximum(m_prev, jnp.max(s, axis=0, keepdims=True))
                a = jnp.exp2(m_prev - m_new)
                p = jnp.exp2(s - m_new).astype(BF16)
                acc_sc[i, :, qs_] = a * acc_sc[i, :, qs_] + _dot(vtb, p)
                m_sc[i, :, qs_] = m_new

    n_wide = (qi * tq) // tk
    n_left = qi - n_wide * (tk // tq)

    def fold(start, width, masked, budget):
        safe = None
        for hh in range(n_h):
            head = hg * n_h + hh
            ok = jnp.logical_and(tile_max(start, width, head) <= budget[hh],
                                 tile_max(start, width, n_heads + head) <= v_ok)
            safe = ok if safe is None else jnp.logical_and(safe, ok)

        def keep_offsets():
            step(start, width, masked, False)
            return budget

        def move_offsets():
            step(start, width, masked, True)
            return norm_budget()

        return lax.cond(safe, keep_offsets, move_offsets)

    def wide(j, budget):
        return fold(pl.multiple_of(j * tk, tk), tk, False, budget)

    def left(j, budget):
        return fold(pl.multiple_of(n_wide * tk + j * tq, tq), tq, False, budget)

    budget = tuple(jnp.where(start_high, k_all[hh], -HUGE) for hh in range(n_h))
    budget = lax.fori_loop(0, n_wide, wide, budget)
    budget = lax.fori_loop(0, n_left, left, budget)
    fold(pl.multiple_of(qi * tq, tq), tq, True, budget)

    lam = lam_ref[...]
    lam_full = (jnp.exp(jnp.sum(lam[0:1] * lam[1:2], axis=-1, keepdims=True))
                - jnp.exp(jnp.sum(lam[2:3] * lam[3:4], axis=-1, keepdims=True))
                + lambda_init)
    for hh in range(n_h):
        num = [acc_sc[2 * hh + c, 0:LANES, :] for c in range(2)]
        inv = [1.0 / acc_sc[2 * hh + c, LANES:LANES + 1, :] for c in range(2)]
        ot = num[0] * inv[0] - lam_full * (num[1] * inv[1])
        ot = ot * lax.rsqrt(jnp.mean(ot * ot, axis=0, keepdims=True) + LN_EPS) * g_ref[...]
        o_ref[0, :, hh * LANES:(hh + 1) * LANES] = (ot * (1.0 - lambda_init)).T.astype(BF16)


def _diff_attention(kmax, qt, k, vt, lam, subln_g, lambda_init):
    b, s, d = k.shape
    tq = min(Q_TILE, s)
    tk = max(tq, min(KV_TILE, s))
    assert tk % tq == 0
    n_h = HEADS_PER_STEP
    gw = n_h * LANES
    assert d % gw == 0
    qt_spec = pl.BlockSpec((1, gw, tq), lambda bi, h, qi, _: (bi, h, qi))
    k_spec = pl.BlockSpec((1, s, gw), lambda bi, h, qi, _: (bi, 0, h))
    vt_spec = pl.BlockSpec((1, n_h * V_ROWS, s), lambda bi, h, qi, _: (bi, h, 0))
    o_spec = pl.BlockSpec((1, tq, gw), lambda bi, h, qi, _: (bi, qi, h))
    g_col = subln_g.reshape(LANES, 1)
    grid_spec = pltpu.PrefetchScalarGridSpec(
        num_scalar_prefetch=1,
        grid=(b, d // gw, s // tq),
        in_specs=[_const_spec(lam.shape), _const_spec(g_col.shape), qt_spec, k_spec, vt_spec],
        out_specs=o_spec,
        scratch_shapes=[pltpu.VMEM((2 * n_h, 1, tq), F32),
                        pltpu.VMEM((2 * n_h, V_ROWS, tq), F32)])
    return pl.pallas_call(
        functools.partial(_attn_kernel, tq=tq, tk=tk, n_h=n_h, n_heads=d // LANES,
                          v_ok=2.0 ** (F32_MAX_EXP - 2 - SAFE_EXP) / s, lambda_init=lambda_init),
        grid_spec=grid_spec,
        out_shape=jax.ShapeDtypeStruct((b, s, d), BF16),
        compiler_params=_params(3),
        name="diff_attn",
    )(kmax, lam, g_col, qt, k, vt)


def _rope_lane_tables(seq):
    head_dim = LANES // 2
    rot = head_dim // 4
    half = rot // 2
    lane = jnp.arange(LANES, dtype=jnp.int32) % head_dim
    inv_freq = ROPE_THETA ** (-(2 * (lane % half)).astype(F32) / rot)
    inv_freq = jnp.where(lane < rot, inv_freq, 0.0)
    ang = jnp.arange(seq, dtype=F32)[:, None] * inv_freq[None, :]
    c, s = jnp.cos(ang), jnp.sin(ang)
    sa = jnp.where(lane < half, -s, 0.0)
    sb = jnp.where((lane >= half) & (lane < rot), s, 0.0)
    return c, sa, sb


def _swiglu_res_ln(x, w13_ref, w2_ref, g_ref, b_ref, alpha, fc):
    d_ff = w2_ref.shape[0]
    xb = x.astype(BF16)
    acc = None
    for c0 in range(0, d_ff, fc):
        gate = _dot(xb, w13_ref[:, c0:c0 + fc])
        up = _dot(xb, w13_ref[:, d_ff + c0:d_ff + c0 + fc])
        act = (gate * jax.nn.sigmoid(gate) * up).astype(BF16)
        part = _dot(act, w2_ref[c0:c0 + fc, :])
        acc = part if acc is None else acc + part
    return _layer_norm(alpha * x + acc, g_ref[...], b_ref[...])


def _ffn_kernel(x_ref, w13_ref, w2_ref, g_ref, b_ref, o_ref, *, alpha, fc):
    rows = x_ref.shape[0] // FFN_STREAMS
    for r0 in range(0, x_ref.shape[0], rows):
        o_ref[r0:r0 + rows, :] = _swiglu_res_ln(x_ref[r0:r0 + rows, :], w13_ref, w2_ref,
                                                g_ref, b_ref, alpha, fc)


def _attn_out_ffn_kernel(x_ref, a_ref, wo_ref, g1_ref, b1_ref, w13_ref, w2_ref, g2_ref, b2_ref,
                         o_ref, *, alpha, fc):
    rows = x_ref.shape[0] // FFN_STREAMS
    for r0 in range(0, x_ref.shape[0], rows):
        rs = slice(r0, r0 + rows)
        x1 = _layer_norm(alpha * x_ref[rs, :] + _dot(a_ref[rs, :], wo_ref[...]),
                         g1_ref[...], b1_ref[...])
        o_ref[rs, :] = _swiglu_res_ln(x1, w13_ref, w2_ref, g2_ref, b2_ref, alpha, fc)


def _attn_out_ffn(x, a, wo, g1, b1, w13, w2, g2, b2, alpha):
    t, d = x.shape
    tm = min(FFN_STREAMS * ROW_TILE, t)
    row = pl.BlockSpec((tm, d), lambda i: (i, 0))
    a_spec = pl.BlockSpec((tm, a.shape[1]), lambda i: (i, 0))
    consts = [wo, g1, b1, w13, w2, g2, b2]
    return pl.pallas_call(
        functools.partial(_attn_out_ffn_kernel, alpha=alpha, fc=_ffn_chunk(w2.shape[0])),
        grid=(t // tm,),
        in_specs=[row, a_spec] + [_const_spec(c.shape) for c in consts],
        out_specs=row,
        out_shape=jax.ShapeDtypeStruct((t, d), F32),
        compiler_params=_params(1),
        name="attn_out_ffn",
    )(x, a, *consts)


def _ffn(x, w13, w2, g, b, alpha):
    t, d = x.shape
    tm = min(FFN_STREAMS * ROW_TILE, t)
    row = pl.BlockSpec((tm, d), lambda i: (i, 0))
    return pl.pallas_call(
        functools.partial(_ffn_kernel, alpha=alpha, fc=_ffn_chunk(w2.shape[0])),
        grid=(t // tm,),
        in_specs=[row, _const_spec(w13.shape), _const_spec(w2.shape),
                  _const_spec(g.shape), _const_spec(b.shape)],
        out_specs=row,
        out_shape=jax.ShapeDtypeStruct((t, d), F32),
        compiler_params=_params(1),
        name="ffn",
    )(x, w13, w2, g, b)


def _ffn_chunk(d_ff):
    for c in (256, 128):
        if d_ff % c == 0:
            return c
    return d_ff


def _pw1_glu_kernel(x_ref, w_ref, b_ref, o_ref):
    d = x_ref.shape[1]
    h = _dot(x_ref[...].astype(BF16), w_ref[...]) + b_ref[...]
    o_ref[...] = h[:, :d] * jax.nn.sigmoid(h[:, d:])


def _pw1_glu(x, w, b):
    t, d = x.shape
    tm = min(ROW_TILE, t)
    row = pl.BlockSpec((tm, d), lambda i: (i, 0))
    return pl.pallas_call(
        _pw1_glu_kernel,
        grid=(t // tm,),
        in_specs=[row, _const_spec(w.shape), _const_spec(b.shape)],
        out_specs=row,
        out_shape=jax.ShapeDtypeStruct((t, d), F32),
        compiler_params=_params(1),
        name="conv_pw1_glu",
    )(x, w, b)


def _conv_kernel(x_ref, h_ref, halo_ref, dww_ref, dwb_ref, lng_ref, lnb_ref, w_ref, wb_ref,
                 g_ref, b_ref, o_ref, buf, shift_ref, *, alpha, tiles_per_seq):
    tm = x_ref.shape[0]
    width = dww_ref.shape[0]
    first = (pl.program_id(0) % tiles_per_seq) == 0
    halo = halo_ref[...]
    buf[0:CONV_HALO, :] = jnp.where(first, jnp.zeros_like(halo), halo)
    buf[CONV_HALO:, :] = h_ref[...]
    acc = jnp.zeros(h_ref.shape, F32) + dwb_ref[...]
    sub = 8
    n_a = (width + sub - 1) // sub
    top = sub * (n_a - 1)
    assert top + sub <= CONV_HALO
    for b in range(sub):
        if b:
            shift_ref[...] = buf[pl.ds(CONV_HALO - top - b, tm + top), :]
            src, base = shift_ref, top
        else:
            src, base = buf, CONV_HALO
        for a in range(n_a):
            j = sub * a + b
            if j < width:
                acc = acc + (src[pl.ds(base - sub * a, tm), :]
                             * dww_ref[width - 1 - j:width - j, :])
    y = _layer_norm(acc, lng_ref[...], lnb_ref[...])
    y = (y * jax.nn.sigmoid(y)).astype(BF16)
    y = _dot(y, w_ref[...]) + wb_ref[...]
    o_ref[...] = _layer_norm(alpha * x_ref[...] + y, g_ref[...], b_ref[...])


def _conv_tail(x, h, dw_w, dw_b, ln_g, ln_b, pw2_w, pw2_b, g, b, alpha, seq):
    t, d = x.shape
    tm = min(ROW_TILE, seq)
    tiles_per_seq = seq // tm
    ratio = tm // CONV_HALO
    row = pl.BlockSpec((tm, d), lambda i: (i, 0))
    halo = pl.BlockSpec((CONV_HALO, d), lambda i: (jnp.maximum(i * ratio - 1, 0), 0))
    consts = [dw_w, dw_b, ln_g, ln_b, pw2_w, pw2_b, g, b]
    return pl.pallas_call(
        functools.partial(_conv_kernel, alpha=alpha, tiles_per_seq=tiles_per_seq),
        grid=(t // tm,),
        in_specs=[row, row, halo] + [_const_spec(c.shape) for c in consts],
        out_specs=row,
        out_shape=jax.ShapeDtypeStruct((t, d), F32),
        scratch_shapes=[pltpu.VMEM((CONV_HALO + tm, d), F32),
                        pltpu.VMEM((tm + 8 * ((dw_w.shape[0] + 7) // 8 - 1), d), F32)],
        compiler_params=_params(1),
        name="conv_tail",
    )(x, h, h, *consts)


def _pool_kernel(x_ref, halo_ref, w_ref, pb_ref, ps_ref, g_ref, b_ref, o_ref, buf, *,
                 alpha, tiles_per_seq):
    tm, d = x_ref.shape
    n_groups = w_ref.shape[0]
    gw = d // n_groups
    tile = pl.program_id(0) % tiles_per_seq
    x = x_ref[...]
    halo = halo_ref[...]
    buf[0:POOL_HALO, :] = jnp.where(tile == 0, jnp.zeros_like(halo), halo)
    buf[POOL_HALO:, :] = x
    pos = tile * tm + lax.broadcasted_iota(jnp.int32, (tm, 1), 0)
    ys = []
    for gi, win in enumerate(POOL_WINDOWS):
        cols = slice(gi * gw, (gi + 1) * gw)
        tot = x[:, cols]
        for j in range(1, win):
            tot = tot + buf[pl.ds(POOL_HALO - j, tm), cols]
        count = jnp.minimum(pos + 1, win).astype(F32)
        pooled = (tot / count - x[:, cols]).astype(BF16)
        ys.append(_dot(pooled, w_ref[gi]))
    y = (jnp.concatenate(ys, axis=1) + pb_ref[...]) * ps_ref[...]
    o_ref[...] = _layer_norm(alpha * x + y, g_ref[...], b_ref[...])


def _pool_layer(x, pool_w, pool_b, pool_scale, g, b, alpha, seq):
    t, d = x.shape
    tm = min(ROW_TILE, seq)
    tiles_per_seq = seq // tm
    ratio = tm // POOL_HALO
    row = pl.BlockSpec((tm, d), lambda i: (i, 0))
    halo = pl.BlockSpec((POOL_HALO, d), lambda i: (jnp.maximum(i * ratio - 1, 0), 0))
    consts = [pool_w, pool_b, pool_scale, g, b]
    return pl.pallas_call(
        functools.partial(_pool_kernel, alpha=alpha, tiles_per_seq=tiles_per_seq),
        grid=(t // tm,),
        in_specs=[row, halo] + [_const_spec(c.shape) for c in consts],
        out_specs=row,
        out_shape=jax.ShapeDtypeStruct((t, d), F32),
        scratch_shapes=[pltpu.VMEM((POOL_HALO + tm, d), F32)],
        compiler_params=_params(1),
        name="pool_layer",
    )(x, x, *consts)


def kernel(x, attn_wqkv, attn_wo, attn_lambda, attn_subln_g, conv_pw1_w, conv_pw1_b, conv_dw_w, conv_dw_b, conv_ln_g, conv_ln_b, conv_pw2_w, conv_pw2_b, pool_w, pool_b, pool_scale, ffn_w13, ffn_w2, ln1_g, ln1_b, ln2_g, ln2_b):
    bsz, seq, d = x.shape
    depth = ffn_w13.shape[0]
    d_ff = ffn_w2.shape[1]
    alpha = float((2 * depth) ** 0.25)
    head_dim = d // N_HEADS // 2
    assert 2 * head_dim == LANES and seq % min(ROW_TILE, seq) == 0
    row2 = lambda v: v.reshape(1, -1)

    cos_t, sa_t, sb_t = _rope_lane_tables(seq)

    h = x.reshape(bsz * seq, d)
    for i in range(depth):
        mixer, j = i % N_MIXERS, i // N_MIXERS
        g1, b1 = row2(ln1_g[i]), row2(ln1_b[i])
        if mixer == 0:
            lambda_init = 0.8 - 0.6 * math.exp(-0.3 * i)
            qt, k, vt, kmax = _qkv_rope(h, attn_wqkv[j].astype(BF16), cos_t, sa_t, sb_t, bsz, seq,
                                        head_dim ** -0.5 * math.log2(math.e))
            tile_stats = kmax[:, :2, :d // LANES]
            k_all = jnp.max(tile_stats[:, 0, :].reshape(bsz, -1, d // LANES), axis=1)
            stats = jnp.concatenate([tile_stats.reshape(-1), k_all.reshape(-1)])
            o = _diff_attention(stats, qt, k.reshape(bsz, seq, d), vt,
                                attn_lambda[j], attn_subln_g[j], lambda_init)
            h = _attn_out_ffn(h, o.reshape(bsz * seq, d), attn_wo[j].astype(BF16), g1, b1,
                              ffn_w13[i].astype(BF16), ffn_w2[i].astype(BF16),
                              row2(ln2_g[i]), row2(ln2_b[i]), alpha)
            continue
        elif mixer == 1:
            glu = _pw1_glu(h, conv_pw1_w[j].astype(BF16), row2(conv_pw1_b[j]))
            h = _conv_tail(h, glu, conv_dw_w[j], row2(conv_dw_b[j]), row2(conv_ln_g[j]),
                           row2(conv_ln_b[j]), conv_pw2_w[j].astype(BF16), row2(conv_pw2_b[j]),
                           g1, b1, alpha, seq)
        else:
            h = _pool_layer(h, pool_w[j].astype(BF16), row2(pool_b[j]), row2(pool_scale[j]),
                            g1, b1, alpha, seq)
        h = _ffn(h, ffn_w13[i].astype(BF16), ffn_w2[i].astype(BF16),
                 row2(ln2_g[i]), row2(ln2_b[i]), alpha)
    return h.reshape(bsz, seq, d)
```
